```python
import jax, jax.numpy as jnp
from jax import lax
import numpy as np

D_MODEL = 1024
BATCH = 8
SEQ = 4096
DEPTH = 4

D_MIX = 2 * D_MODEL
SSD_WIDTH = D_MIX // 2
SSD_HEAD_DIM = 64
SSD_HEADS = SSD_WIDTH // SSD_HEAD_DIM
SSD_GROUPS = 2
SSD_HPG = SSD_HEADS // SSD_GROUPS
SSD_STATE = 128
SSD_CONV = 5
SSD_CHUNK = 64
XBC_WIDTH = SSD_WIDTH + 2 * SSD_GROUPS * SSD_STATE
HG_WIDTH = D_MIX - SSD_WIDTH
HG_EXPAND = 128
HG_HEADS = HG_WIDTH // HG_EXPAND
HG_DK = HG_EXPAND
HG_DV = HG_WIDTH // HG_HEADS
HG_CHUNK = 32
D_FF = 4 * D_MODEL
N_MOD = 6
EPS = 1e-6
LB_FLOOR = 1e-30

IN_SIZES = (SSD_WIDTH, XBC_WIDTH, SSD_HEADS, HG_WIDTH, HG_WIDTH, HG_WIDTH, HG_WIDTH, HG_WIDTH)
D_IN_PROJ = sum(IN_SIZES)
IN_SPLITS = tuple(np.cumsum(IN_SIZES)[:-1].tolist())

kernel_name = "hybrid_ssd_hgrn2_adaln_encoder"


def rmsnorm(x, w):
    xf = x.astype(jnp.float32)
    y = xf * lax.rsqrt(jnp.mean(xf * xf, axis=-1, keepdims=True) + EPS)
    return (y * w.astype(jnp.float32)).astype(x.dtype)


def to_chunks(a, chunk):
    b, l = a.shape[:2]
    return jnp.moveaxis(a.reshape(b, l // chunk, chunk, *a.shape[2:]), 1, 0)


def from_chunks(a):
    n, b, chunk = a.shape[:3]
    return jnp.moveaxis(a, 0, 1).reshape(b, n * chunk, *a.shape[3:])


def flip_seq(a):
    return jnp.flip(a, axis=1)


def masked_decay(seg, tri):
    return jnp.where(tri, jnp.exp(jnp.where(tri, seg, 0.0)), 0.0)


def depthwise_conv_centred(x, w):
    k = w.shape[0]
    return lax.conv_general_dilated(
        x, w[:, None, :].astype(x.dtype), window_strides=(1,),
        padding=[(k // 2, k // 2)], dimension_numbers=("NWC", "WIO", "NWC"),
        feature_group_count=x.shape[-1])


def ssd_direction(xh, bg, cg, dt, a_neg):
    bsz = xh.shape[0]
    log_a = dt * a_neg
    xs = tuple(to_chunks(t, SSD_CHUNK) for t in (xh, bg, cg, dt, log_a))
    tri = jnp.tril(jnp.ones((SSD_CHUNK, SSD_CHUNK), dtype=bool))[None, :, :, None, None]

    def step(state, inp):
        xc, bc, cc, dtc, ac = inp
        cum = jnp.cumsum(ac, axis=1)
        seg = cum[:, :, None] - cum[:, None, :]
        decay = masked_decay(seg, tri)
        scores = jnp.einsum("btgn,bsgn->btsg", cc, bc)
        w = scores[..., None] * decay * dtc[:, None]
        y_intra = jnp.einsum("btsgh,bsghp->btghp", w, xc)
        y_inter = jnp.einsum("btgn,bghpn->btghp", cc, state) * jnp.exp(cum)[..., None]
        w_state = jnp.exp(cum[:, -1:] - cum) * dtc
        new_state = (jnp.exp(cum[:, -1])[..., None, None] * state
                     + jnp.einsum("bsgh,bsghp,bsgn->bghpn", w_state, xc, bc))
        return new_state, y_intra + y_inter

    init = jnp.zeros((bsz, SSD_GROUPS, SSD_HPG, SSD_HEAD_DIM, SSD_STATE), jnp.float32)
    _, ys = lax.scan(step, init, xs)
    return from_chunks(ys)


def ssd_mixer(z, xbc, dt_raw, conv_w, conv_b, dt_bias, a_log, d_skip, norm_w):
    f32 = jnp.float32
    bsz, seqlen = z.shape[:2]
    xbc = jax.nn.silu(depthwise_conv_centred(xbc, conv_w) + conv_b.astype(xbc.dtype))
    xm, bm, cm = jnp.split(xbc, (SSD_WIDTH, SSD_WIDTH + SSD_GROUPS * SSD_STATE), axis=-1)
    xh = xm.astype(f32).reshape(bsz, seqlen, SSD_GROUPS, SSD_HPG, SSD_HEAD_DIM)
    bg = bm.astype(f32).reshape(bsz, seqlen, SSD_GROUPS, SSD_STATE)
    cg = cm.astype(f32).reshape(bsz, seqlen, SSD_GROUPS, SSD_STATE)
    y = xh * d_skip.astype(f32).reshape(SSD_GROUPS, SSD_HPG)[..., None]
    for d in range(2):
        dt = jax.nn.softplus(dt_raw.astype(f32) + dt_bias[d].astype(f32))
        dt = dt.reshape(bsz, seqlen, SSD_GROUPS, SSD_HPG)
        a_neg = -jnp.exp(a_log[d].astype(f32)).reshape(SSD_GROUPS, SSD_HPG)
        if d == 0:
            y = y + ssd_direction(xh, bg, cg, dt, a_neg)
        else:
            y = y + flip_seq(ssd_direction(flip_seq(xh), flip_seq(bg), flip_seq(cg), flip_seq(dt), a_neg))
    y = y.reshape(bsz, seqlen, SSD_WIDTH)
    return rmsnorm(y * jax.nn.silu(z.astype(f32)), norm_w)


def hgrn2_direction(q, log_f, k, v):
    bsz = q.shape[0]
    xs = tuple(to_chunks(t, HG_CHUNK) for t in (q, log_f, k, v))
    tri = jnp.tril(jnp.ones((HG_CHUNK, HG_CHUNK), dtype=bool))[None, :, :, None, None]

    def step(state, inp):
        qc, gc, kc, vc = inp
        cum = jnp.cumsum(gc, axis=1)
        seg = cum[:, :, None] - cum[:, None, :]
        decay = masked_decay(seg, tri)
        attn = jnp.einsum("bthk,btshk,bshk->btsh", qc, decay, kc)
        y_intra = jnp.einsum("btsh,bshv->bthv", attn, vc)
        y_inter = jnp.einsum("bthk,bhkv->bthv", qc * jnp.exp(cum), state)
        cum_end = cum[:, -1]
        new_state = (jnp.exp(cum_end)[..., None] * state
                     + jnp.einsum("bshk,bshv->bhkv", kc * jnp.exp(cum_end[:, None] - cum), vc))
        return new_state, y_intra + y_inter

    init = jnp.zeros((bsz, HG_HEADS, HG_DK, HG_DV), jnp.float32)
    _, ys = lax.scan(step, init, xs)
    return from_chunks(ys)


def hgrn2_mixer(q, f_fwd, f_bwd, i, g, lb, norm_w):
    f32 = jnp.float32
    bsz, seqlen = q.shape[:2]
    heads = lambda t: t.astype(f32).reshape(bsz, seqlen, HG_HEADS, -1)
    qh, vh = heads(q), heads(i)
    y = jnp.zeros((bsz, seqlen, HG_HEADS, HG_DV), f32)
    for d, f_raw in enumerate((f_fwd, f_bwd)):
        lbd = lb[d].reshape(HG_HEADS, HG_DK)
        fh = heads(f_raw)
        log_f = jnp.logaddexp(jnp.log(jnp.maximum(lbd, LB_FLOOR)),
                              jnp.log1p(-lbd) + jax.nn.log_sigmoid(fh))
        kh = (1.0 - lbd) * jax.nn.sigmoid(-fh)
        if d == 0:
            y = y + hgrn2_direction(qh, log_f, kh, vh)
        else:
            y = y + flip_seq(hgrn2_direction(flip_seq(qh), flip_seq(log_f), flip_seq(kh), flip_seq(vh)))
    y = rmsnorm(y, norm_w).reshape(bsz, seqlen, HG_WIDTH)
    return y * jax.nn.silu(g.astype(f32))


def math_log(v):
    return float(np.log(v))


def setup_inputs(seed: int = 0) -> dict:
    key = jax.random.key(seed)
    ks = jax.random.split(key, 20)
    nrm = jax.random.normal
    f32 = jnp.float32
    x = nrm(ks[0], (BATCH, SEQ, D_MODEL), f32)
    c = nrm(ks[1], (BATCH, D_MODEL), f32)
    ada_w = nrm(ks[2], (DEPTH, D_MODEL, N_MOD * D_MODEL), f32) * (0.5 * D_MODEL ** -0.5)
    ada_b = nrm(ks[3], (DEPTH, N_MOD * D_MODEL), f32) * 0.02
    norm_mix_w = 1.0 + 0.02 * nrm(ks[4], (DEPTH, D_MODEL), f32)
    norm_mlp_w = 1.0 + 0.02 * nrm(ks[5], (DEPTH, D_MODEL), f32)
    w_in = nrm(ks[6], (DEPTH, D_MODEL, D_IN_PROJ), f32) * D_MODEL ** -0.5
    conv_w = nrm(ks[7], (DEPTH, SSD_CONV, XBC_WIDTH), f32) * SSD_CONV ** -0.5
    conv_b = nrm(ks[8], (DEPTH, XBC_WIDTH), f32) * 0.02
    dt0 = jnp.exp(jax.random.uniform(ks[9], (DEPTH, 2, SSD_HEADS), f32,
                                     minval=math_log(1e-3), maxval=math_log(1e-1)))
    dt_bias = dt0 + jnp.log(-jnp.expm1(-dt0))
    a_log = jnp.log(jax.random.uniform(ks[10], (DEPTH, 2, SSD_HEADS), f32, minval=1.0, maxval=16.0))
    d_skip = 1.0 + 0.1 * nrm(ks[11], (DEPTH, SSD_HEADS), f32)
    ssd_norm_w = 1.0 + 0.02 * nrm(ks[12], (DEPTH, SSD_WIDTH), f32)
    hg_lower_bounds = 0.5 * nrm(ks[13], (2, DEPTH, HG_WIDTH), f32)
    hg_norm_w = 1.0 + 0.02 * nrm(ks[14], (DEPTH, HG_DV), f32)
    w_out = nrm(ks[15], (DEPTH, D_MIX, D_MODEL), f32) * D_MIX ** -0.5
    w_up = nrm(ks[16], (DEPTH, D_MODEL, D_FF), f32) * D_MODEL ** -0.5
    w_down = nrm(ks[17], (DEPTH, D_FF, D_MODEL), f32) * D_FF ** -0.5
    final_norm_w = 1.0 + 0.02 * nrm(ks[18], (D_MODEL,), f32)
    return {"x": x, "c": c, "ada_w": ada_w, "ada_b": ada_b, "norm_mix_w": norm_mix_w,
            "norm_mlp_w": norm_mlp_w, "w_in": w_in, "conv_w": conv_w, "conv_b": conv_b,
            "dt_bias": dt_bias, "a_log": a_log, "d_skip": d_skip, "ssd_norm_w": ssd_norm_w,
            "hg_lower_bounds": hg_lower_bounds, "hg_norm_w": hg_norm_w, "w_out": w_out,
            "w_up": w_up, "w_down": w_down, "final_norm_w": final_norm_w}


def reference(x, c, ada_w, ada_b, norm_mix_w, norm_mlp_w, w_in, conv_w, conv_b, dt_bias, a_log,
              d_skip, ssd_norm_w, hg_lower_bounds, hg_norm_w, w_out, w_up, w_down, final_norm_w):
    p = jax.nn.softmax(hg_lower_bounds.astype(jnp.float32), axis=1)
    lower_bounds = jnp.cumsum(p, axis=1) - p[:, :1]
    c_act = jax.nn.silu(c)
    for l in range(DEPTH):
        mod = c_act @ ada_w[l] + ada_b[l]
        shift1, scale1, gate1, shift2, scale2, gate2 = jnp.split(mod[:, None, :], N_MOD, axis=-1)
        h = rmsnorm(x, norm_mix_w[l]) * (1.0 + scale1) + shift1
        proj = h @ w_in[l]
        z, xbc, dt_raw, q, f_fwd, f_bwd, i, g = jnp.split(proj, IN_SPLITS, axis=-1)
        y_ssd = ssd_mixer(z, xbc, dt_raw, conv_w[l], conv_b[l], dt_bias[l], a_log[l],
                          d_skip[l], ssd_norm_w[l])
        y_hg = hgrn2_mixer(q, f_fwd, f_bwd, i, g, lower_bounds[:, l], hg_norm_w[l])
        y = jnp.concatenate([y_ssd, y_hg], axis=-1).astype(x.dtype) @ w_out[l]
        x = x + gate1 * y
        h = rmsnorm(x, norm_mlp_w[l]) * (1.0 + scale2) + shift2
        x = x + gate2 * (jnp.square(jax.nn.relu(h @ w_up[l])) @ w_down[l])
    return rmsnorm(x, final_norm_w)
```

```python
import functools

import numpy as np
import jax
import jax.numpy as jnp
from jax import lax
from jax.experimental import pallas as pl
from jax.experimental.pallas import tpu as pltpu

F32 = jnp.float32
BF16 = jnp.bfloat16

D_MODEL = 1024
N_MOD = 6
SSD_WIDTH = 1024
SSD_HEAD_DIM = 64
SSD_HEADS = 16
SSD_GROUPS = 2
SSD_STATE = 128
SSD_CONV = 5
XBC_WIDTH = SSD_WIDTH + 2 * SSD_GROUPS * SSD_STATE
HG_WIDTH = 1024
HG_HEADS = 8
HG_DK = 128
HG_DV = 128
D_FF = 4096
EPS = 1e-6
LB_FLOOR = 1e-30

LANES = 128
HALO = 16
VMEM_LIMIT = 56 * 1024 * 1024
NEG = -1e30

PROJ_TM = 512
PROJ_NC = 512
SSD_Q = 128
HG_Q = 128


def _sigmoid(v):
    return 1.0 / (1.0 + jnp.exp(-v))


def _silu(v):
    return v * _sigmoid(v)


def _softplus(v):
    return jnp.maximum(v, 0.0) + jnp.log(1.0 + jnp.exp(-jnp.abs(v)))


def _dot(a, b):
    return jnp.dot(a, b, preferred_element_type=F32)


def _dot_nt(a, b):
    return lax.dot_general(a, b, (((1,), (1,)), ((), ())), preferred_element_type=F32)


def _split2(v):
    hi = v.astype(BF16)
    lo = (v - hi.astype(F32)).astype(BF16)
    return hi, lo


def _split3(v):
    hi = v.astype(BF16)
    r = v - hi.astype(F32)
    mid = r.astype(BF16)
    lo = (r - mid.astype(F32)).astype(BF16)
    return hi, mid, lo


def _const_spec(shape):
    nd = len(shape)
    return pl.BlockSpec(shape, lambda *_: (0,) * nd, pipeline_mode=pl.Buffered(1))


def _mod_kernel(c_ref, w_ref, b_ref, o_ref):
    ca = _silu(c_ref[...]).astype(BF16)
    o_ref[0] = _dot(ca, w_ref[0].astype(BF16)) + b_ref[0]


def _modulation(c, ada_w, ada_b):
    depth, d, n = ada_w.shape
    bsz = c.shape[0]
    nb = n // d
    return pl.pallas_call(
        _mod_kernel,
        grid=(depth, nb),
        in_specs=[
            pl.BlockSpec((bsz, d), lambda l, j: (0, 0)),
            pl.BlockSpec((1, d, d), lambda l, j: (l, 0, j)),
            pl.BlockSpec((1, 1, d), lambda l, j: (l, 0, j)),
        ],
        out_specs=pl.BlockSpec((1, bsz, d), lambda l, j: (l, 0, j)),
        out_shape=jax.ShapeDtypeStruct((depth, bsz, n), F32),
        compiler_params=pltpu.CompilerParams(
            dimension_semantics=("arbitrary", "arbitrary"), vmem_limit_bytes=VMEM_LIMIT),
        name="adaln_modulation",
    )(c, ada_w, ada_b.reshape(depth, 1, n))


def _lb_kernel(lb_ref, o_ref):
    depth = lb_ref.shape[1]
    for d in range(2):
        rows = [lb_ref[d, l:l + 1, :] for l in range(depth)]
        m = rows[0]
        for r in rows[1:]:
            m = jnp.maximum(m, r)
        es = [jnp.exp(r - m) for r in rows]
        tot = es[0]
        for e in es[1:]:
            tot = tot + e
        inv = 1.0 / tot
        ps = [e * inv for e in es]
        run = ps[0]
        for l in range(depth):
            if l > 0:
                run = run + ps[l]
            lb = run - ps[0]
            o_ref[l, 2 * d:2 * d + 1, :] = jnp.maximum(lb, LB_FLOOR)
            o_ref[l, 2 * d + 1:2 * d + 2, :] = 1.0 - lb
    o_ref[:, 4:8, :] = jnp.zeros((depth, 4, o_ref.shape[2]), F32)


def _lower_bounds(hg_lower_bounds):
    _, depth, w = hg_lower_bounds.shape
    return pl.pallas_call(
        _lb_kernel,
        out_shape=jax.ShapeDtypeStruct((depth, 8, w), F32),
        name="hgrn2_lower_bounds",
    )(hg_lower_bounds.astype(F32))


def _inproj_kernel(x_ref, xp_ref, xn_ref, mod_ref, nw_ref, w_ref, wdt_ref, cw_ref, cb_ref,
                   z_ref, xbc_ref, dt_ref, q_ref, ff_ref, fb_ref, iv_ref, g_ref,
                   h_scr, xe_scr, *, tm):
    i = pl.program_id(1)
    n_i = pl.num_programs(1)
    d = D_MODEL
    shift = mod_ref[0, :, 0:d]
    scale = mod_ref[0, :, d:2 * d]
    nw = nw_ref[...]

    def norm_mod(xv):
        ms = jnp.mean(xv * xv, axis=-1, keepdims=True)
        y = xv * lax.rsqrt(ms + EPS) * nw
        return (y * (1.0 + scale) + shift).astype(BF16)

    h_scr[0:HALO, :] = norm_mod(xp_ref[...])
    h_scr[HALO:HALO + tm, :] = norm_mod(x_ref[...])
    h_scr[HALO + tm:2 * HALO + tm, :] = norm_mod(xn_ref[...])

    hm = h_scr[HALO:HALO + tm, :]
    nc = PROJ_NC

    def plain(out_ref, col0, width):
        for c in range(0, width, nc):
            acc = _dot(hm, w_ref[:, col0 + c:col0 + c + nc])
            out_ref[:, c:c + nc] = acc.astype(out_ref.dtype)

    plain(z_ref, 0, SSD_WIDTH)
    col = SSD_WIDTH

    he = h_scr[...]
    for c in range(0, XBC_WIDTH, nc):
        xe_scr[...] = _dot(he, w_ref[:, col + c:col + c + nc])

        @pl.when(i == 0)
        def _():
            xe_scr[0:HALO, :] = jnp.zeros((HALO, nc), F32)

        @pl.when(i == n_i - 1)
        def _():
            xe_scr[HALO + tm:2 * HALO + tm, :] = jnp.zeros((HALO, nc), F32)

        acc = jnp.zeros((tm, nc), F32) + cb_ref[:, c:c + nc]
        for k in range(SSD_CONV):
            off = HALO - SSD_CONV // 2 + k
            acc = acc + cw_ref[k:k + 1, c:c + nc] * xe_scr[off:off + tm, :]
        xbc_ref[:, c:c + nc] = _silu(acc).astype(xbc_ref.dtype)
    col += XBC_WIDTH

    for out_ref in (q_ref, ff_ref, fb_ref, iv_ref, g_ref):
        plain(out_ref, col, HG_WIDTH)
        col += HG_WIDTH

    dt_ref[...] = _dot(hm, wdt_ref[...])


def _in_proj(x2, mod_l, nw, w_main, w_dt, conv_w, conv_b, *, bsz, seqlen):
    tm = PROJ_TM
    nt = seqlen // tm
    hb = tm // HALO
    n_halo_blocks = bsz * seqlen // HALO
    tok = bsz * seqlen
    d = D_MODEL

    def x_map(b, i):
        return (b * nt + i, 0)

    def xp_map(b, i):
        return (jnp.maximum((b * nt + i) * hb - 1, 0), 0)

    def xn_map(b, i):
        return (jnp.minimum((b * nt + i + 1) * hb, n_halo_blocks - 1), 0)

    def tile(width, dtype):
        return pl.BlockSpec((tm, width), x_map), jax.ShapeDtypeStruct((tok, width), dtype)

    outs = [tile(SSD_WIDTH, BF16), tile(XBC_WIDTH, BF16), tile(LANES, F32)] + [tile(HG_WIDTH, BF16)] * 5
    return pl.pallas_call(
        functools.partial(_inproj_kernel, tm=tm),
        grid=(bsz, nt),
        in_specs=[
            pl.BlockSpec((tm, d), x_map),
            pl.BlockSpec((HALO, d), xp_map),
            pl.BlockSpec((HALO, d), xn_map),
            pl.BlockSpec((1, 1, N_MOD * d), lambda b, i: (b, 0, 0)),
            _const_spec((1, d)),
            _const_spec(w_main.shape),
            _const_spec(w_dt.shape),
            _const_spec(conv_w.shape),
            _const_spec(conv_b.shape),
        ],
        out_specs=[o[0] for o in outs],
        out_shape=[o[1] for o in outs],
        scratch_shapes=[
            pltpu.VMEM((tm + 2 * HALO, d), BF16),
            pltpu.VMEM((tm + 2 * HALO, PROJ_NC), F32),
        ],
        compiler_params=pltpu.CompilerParams(
            dimension_semantics=("arbitrary", "arbitrary"), vmem_limit_bytes=VMEM_LIMIT),
        name="in_proj",
    )(x2, x2, x2, mod_l, nw, w_main, w_dt, conv_w, conv_b)


def _ssd_kernel(z_ref, xbc_ref, dt_ref, prm_ref, dexp_ref, nw_ref, e_ref, o_ref,
                yf_scr, st_scr, *, n_chunks, q):
    j = pl.program_id(1)
    hpg = SSD_HEADS // SSD_GROUPS
    gw = hpg * SSD_HEAD_DIM

    @pl.when((j == 0) | (j == n_chunks))
    def _():
        st_scr[...] = jnp.zeros(st_scr.shape, F32)

    def chunk(direction):
        c = j if direction == 0 else 2 * n_chunks - 1 - j
        row0 = pl.multiple_of(c * q, q)
        xbc = xbc_ref[...]
        x_bf = xbc[:, 0:SSD_WIDTH]
        x_f = x_bf.astype(F32)
        bias = prm_ref[direction:direction + 1, :]
        a_neg = -jnp.exp(prm_ref[2 + direction:3 + direction, :])
        dtv = _softplus(dt_ref[...] + bias)
        a = dtv * a_neg

        rows = lax.broadcasted_iota(jnp.int32, (q, q), 0)
        cols = lax.broadcasted_iota(jnp.int32, (q, q), 1)
        tri = (cols <= rows) if direction == 0 else (cols >= rows)
        tri_bf = jnp.where(tri, 1.0, 0.0).astype(BF16)
        a1, a2, a3 = _split3(a)
        cum = _dot(tri_bf, a1) + _dot(tri_bf, a2) + _dot(tri_bf, a3)
        end = q - 1 if direction == 0 else 0
        cum_end = cum[end:end + 1, :]
        ecum = jnp.exp(cum)
        wst = jnp.exp(cum_end - cum) * dtv
        cum_t = cum.T
        dt_t = dtv.T

        e_mat = e_ref[...]
        e1, e2 = _split2(ecum)
        ecum_x = _dot(e1, e_mat) + _dot(e2, e_mat)
        wst_x = _dot(wst.astype(BF16), e_mat)
        xw = (x_f * wst_x).astype(BF16)

        lane = lax.broadcasted_iota(jnp.int32, (q, LANES), 1)
        low_half = lane < SSD_HEAD_DIM

        y_parts = []
        for g in range(SSD_GROUPS):
            b_bf = xbc[:, SSD_WIDTH + g * SSD_STATE:SSD_WIDTH + (g + 1) * SSD_STATE]
            c0 = SSD_WIDTH + SSD_GROUPS * SSD_STATE + g * SSD_STATE
            c_bf = xbc[:, c0:c0 + SSD_STATE]
            scores = _dot_nt(c_bf, b_bf)

            st = st_scr[g]
            y_inter = _dot(c_bf, st.astype(BF16)) * ecum_x[:, g * gw:(g + 1) * gw]
            b_t = b_bf.astype(F32).T.astype(BF16)
            dec = ecum_x[end:end + 1, g * gw:(g + 1) * gw]
            st_scr[g] = st * dec + _dot(b_t, xw[:, g * gw:(g + 1) * gw])

            for p in range(hpg // 2):
                ws = []
                for h in (g * hpg + 2 * p, g * hpg + 2 * p + 1):
                    seg = cum[:, h:h + 1] - cum_t[h:h + 1, :]
                    decay = jnp.exp(jnp.where(tri, seg, NEG))
                    ws.append((scores * decay * dt_t[h:h + 1, :]).astype(BF16))
                w_pair = jnp.concatenate(ws, axis=1)
                l0 = g * gw + p * LANES
                xp = x_bf[:, l0:l0 + LANES]
                zero = jnp.zeros_like(xp)
                x_bd = jnp.concatenate([jnp.where(low_half, xp, zero),
                                        jnp.where(low_half, zero, xp)], axis=0)
                y_parts.append(_dot(w_pair, x_bd) + y_inter[:, p * LANES:(p + 1) * LANES])
        y = jnp.concatenate(y_parts, axis=1)

        if direction == 0:
            yf_scr[pl.ds(row0, q), :] = y + x_f * dexp_ref[...]
        else:
            y = y + yf_scr[pl.ds(row0, q), :]
            y = y * _silu(z_ref[...].astype(F32))
            ms = jnp.mean(y * y, axis=-1, keepdims=True)
            o_ref[...] = (y * lax.rsqrt(ms + EPS) * nw_ref[...]).astype(o_ref.dtype)

    @pl.when(j < n_chunks)
    def _():
        chunk(0)

    @pl.when(j >= n_chunks)
    def _():
        chunk(1)


def _ssd_mixer(z, xbc, dt, prm, dexp, nw, e_mat, *, bsz, seqlen):
    q = SSD_Q
    n_chunks = seqlen // q
    tok = bsz * seqlen

    def both(b, j):
        return (b * n_chunks + jnp.where(j < n_chunks, j, 2 * n_chunks - 1 - j), 0)

    def second(b, j):
        return (b * n_chunks + jnp.where(j < n_chunks, n_chunks - 1, 2 * n_chunks - 1 - j), 0)

    return pl.pallas_call(
        functools.partial(_ssd_kernel, n_chunks=n_chunks, q=q),
        grid=(bsz, 2 * n_chunks),
        in_specs=[
            pl.BlockSpec((q, SSD_WIDTH), second),
            pl.BlockSpec((q, XBC_WIDTH), both),
            pl.BlockSpec((q, LANES), both),
            _const_spec(prm.shape),
            _const_spec(dexp.shape),
            _const_spec(nw.shape),
            _const_spec(e_mat.shape),
        ],
        out_specs=pl.BlockSpec((q, SSD_WIDTH), second),
        out_shape=jax.ShapeDtypeStruct((tok, SSD_WIDTH), BF16),
        scratch_shapes=[
            pltpu.VMEM((seqlen, SSD_WIDTH), F32),
            pltpu.VMEM((SSD_GROUPS, SSD_STATE, SSD_WIDTH // SSD_GROUPS), F32),
        ],
        compiler_params=pltpu.CompilerParams(
            dimension_semantics=("arbitrary", "arbitrary"), vmem_limit_bytes=VMEM_LIMIT),
        name="ssd_mixer",
    )(z, xbc, dt, prm, dexp, nw, e_mat)


def _hg_level_mats(q):
    n_lev = int(np.log2(q))
    mats = np.zeros((2, n_lev + 2, q, q), np.float32)
    t = np.arange(q)[:, None]
    r = np.arange(q)[None, :]
    for lev in range(n_lev):
        half = 1 << lev
        same = (t >> (lev + 1)) == (r >> (lev + 1))
        t_up = ((t >> lev) & 1) == 1
        r_up = ((r >> lev) & 1) == 1
        mats[0, lev] = same & ((t_up & r_up & (r <= t)) | (~t_up & ~r_up & (r > t)))
        mats[1, lev] = same & ((~t_up & ~r_up & (r >= t)) | (t_up & r_up & (r < t)))
        del half
    mats[0, n_lev] = r <= t
    mats[0, n_lev + 1] = r > t
    mats[1, n_lev] = r >= t
    mats[1, n_lev + 1] = r < t
    return mats


def _hg_kernel(q_ref, ff_ref, fb_ref, iv_ref, g_ref, lb_ref, nw_ref, m_ref, o_ref,
               yf_scr, st_scr, *, n_chunks, q):
    j = pl.program_id(1)
    n_lev = int(np.log2(q))

    @pl.when((j == 0) | (j == n_chunks))
    def _():
        st_scr[...] = jnp.zeros(st_scr.shape, F32)

    def chunk(direction):
        c = j if direction == 0 else 2 * n_chunks - 1 - j
        row0 = pl.multiple_of(c * q, q)
        f_ref = ff_ref if direction == 0 else fb_ref
        q_bf = q_ref[...]
        q_f = q_bf.astype(F32)
        v_bf = iv_ref[...]
        fh = f_ref[...].astype(F32)
        lbf = lb_ref[0, 2 * direction:2 * direction + 1, :]
        oml = lb_ref[0, 2 * direction + 1:2 * direction + 2, :]
        e = jnp.exp(-jnp.abs(fh))
        r = 1.0 / (1.0 + e)
        er = e * r
        pos = fh >= 0.0
        sig_pos = jnp.where(pos, r, er)
        sig_neg = jnp.where(pos, er, r)
        logf = jnp.log(lbf + oml * sig_pos)
        k_f = oml * sig_neg
        l_hi, l_lo = _split2(logf)

        def seg_exp(idx):
            m = m_ref[direction, idx]
            return jnp.exp(_dot(m, l_hi) + _dot(m, l_lo))

        rows_w = lax.broadcasted_iota(jnp.int32, (q, HG_WIDTH), 0)
        rows = lax.broadcasted_iota(jnp.int32, (q, q), 0)
        cols = lax.broadcasted_iota(jnp.int32, (q, q), 1)

        attn = [None] * HG_HEADS
        k_bf = k_f.astype(BF16)
        diag = rows == cols
        for h in range(HG_HEADS):
            sl = slice(h * HG_DK, (h + 1) * HG_DK)
            attn[h] = jnp.where(diag, _dot_nt(q_bf[:, sl], k_bf[:, sl]), 0.0)
        for lev in range(n_lev):
            upper_w = ((rows_w >> lev) & 1) == 1
            is_query = upper_w if direction == 0 else jnp.logical_not(upper_w)
            xl = (jnp.where(is_query, q_f, k_f) * seg_exp(lev)).astype(BF16)
            same = (rows >> (lev + 1)) == (cols >> (lev + 1))
            row_up = ((rows >> lev) & 1) == 1
            col_up = ((cols >> lev) & 1) == 1
            if direction == 0:
                mask = same & row_up & jnp.logical_not(col_up)
            else:
                mask = same & jnp.logical_not(row_up) & col_up
            for h in range(HG_HEADS):
                sl = slice(h * HG_DK, (h + 1) * HG_DK)
                attn[h] = attn[h] + jnp.where(mask, _dot_nt(xl[:, sl], xl[:, sl]), 0.0)

        eq = seg_exp(n_lev)
        xq = (q_f * eq).astype(BF16)
        xk = (k_f * seg_exp(n_lev + 1)).astype(BF16)
        end = q - 1 if direction == 0 else 0
        dec = eq[end:end + 1, :]

        y_parts = []
        for h in range(HG_HEADS):
            sl = slice(h * HG_DK, (h + 1) * HG_DK)
            st = st_scr[h]
            y_h = _dot(attn[h].astype(BF16), v_bf[:, sl]) + _dot_nt(xq[:, sl], st.astype(BF16))
            v_t = v_bf[:, sl].astype(F32).T.astype(BF16)
            st_scr[h] = st * dec[:, sl] + _dot(v_t, xk[:, sl])
            y_parts.append(y_h)
        y = jnp.concatenate(y_parts, axis=1)

        if direction == 0:
            yf_scr[pl.ds(row0, q), :] = y
        else:
            y = y + yf_scr[pl.ds(row0, q), :]
            gate = _silu(g_ref[...].astype(F32))
            nw = nw_ref[...]
            for h in range(HG_HEADS):
                sl = slice(h * HG_DV, (h + 1) * HG_DV)
                y_h = y[:, sl]
                ms = jnp.mean(y_h * y_h, axis=-1, keepdims=True)
                o_ref[:, sl] = (y_h * lax.rsqrt(ms + EPS) * nw[:, sl] * gate[:, sl]).astype(o_ref.dtype)

    @pl.when(j < n_chunks)
    def _():
        chunk(0)

    @pl.when(j >= n_chunks)
    def _():
        chunk(1)


def _hg_mixer(qp, ff, fb, iv, g, lb_all, layer, nw, level_mats, *, bsz, seqlen):
    q = HG_Q
    n_chunks = seqlen // q
    tok = bsz * seqlen

    def both(b, j):
        return (b * n_chunks + jnp.where(j < n_chunks, j, 2 * n_chunks - 1 - j), 0)

    def first(b, j):
        return (b * n_chunks + jnp.where(j < n_chunks, j, n_chunks - 1), 0)

    def second(b, j):
        return (b * n_chunks + jnp.where(j < n_chunks, n_chunks - 1, 2 * n_chunks - 1 - j), 0)

    blk = lambda m: pl.BlockSpec((q, HG_WIDTH), m)
    return pl.pallas_call(
        functools.partial(_hg_kernel, n_chunks=n_chunks, q=q),
        grid=(bsz, 2 * n_chunks),
        in_specs=[
            blk(both), blk(first), blk(second), blk(both), blk(second),
            pl.BlockSpec((1, 8, HG_WIDTH), lambda b, j: (layer, 0, 0), pipeline_mode=pl.Buffered(1)),
            _const_spec(nw.shape),
            _const_spec(level_mats.shape),
        ],
        out_specs=blk(second),
        out_shape=jax.ShapeDtypeStruct((tok, HG_WIDTH), BF16),
        scratch_shapes=[
            pltpu.VMEM((seqlen, HG_WIDTH), F32),
            pltpu.VMEM((HG_HEADS, HG_DV, HG_DK), F32),
        ],
        compiler_params=pltpu.CompilerParams(
            dimension_semantics=("arbitrary", "arbitrary"), vmem_limit_bytes=VMEM_LIMIT),
        name="hgrn2_mixer",
    )(qp, ff, fb, iv, g, lb_all, nw, level_mats)


def _out_mlp_kernel(ys_ref, yh_ref, x_ref, mod_ref, nw_ref, wo_ref, wu_ref, wd_ref, fw_ref, o_ref,
                    *, final):
    d = D_MODEL
    gate1 = mod_ref[0, :, 2 * d:3 * d]
    shift2 = mod_ref[0, :, 3 * d:4 * d]
    scale2 = mod_ref[0, :, 4 * d:5 * d]
    gate2 = mod_ref[0, :, 5 * d:6 * d]
    y = _dot(ys_ref[...], wo_ref[0:SSD_WIDTH, :]) + _dot(yh_ref[...], wo_ref[SSD_WIDTH:, :])
    x1 = x_ref[...] + gate1 * y
    ms = jnp.mean(x1 * x1, axis=-1, keepdims=True)
    h = ((x1 * lax.rsqrt(ms + EPS) * nw_ref[...]) * (1.0 + scale2) + shift2).astype(BF16)
    acc = jnp.zeros(x1.shape, F32)
    nc = 2 * PROJ_NC
    for c in range(0, D_FF, nc):
        u = jnp.maximum(_dot(h, wu_ref[:, c:c + nc]), 0.0)
        acc = acc + _dot((u * u).astype(BF16), wd_ref[c:c + nc, :])
    x2 = x1 + gate2 * acc
    if final:
        ms2 = jnp.mean(x2 * x2, axis=-1, keepdims=True)
        x2 = x2 * lax.rsqrt(ms2 + EPS) * fw_ref[...]
    o_ref[...] = x2


def _out_mlp(ys, yh, x2, mod_l, nw, w_out, w_up, w_down, fw, *, bsz, seqlen, final):
    tm = PROJ_TM
    nt = seqlen // tm
    d = D_MODEL

    def t_map(b, i):
        return (b * nt + i, 0)

    return pl.pallas_call(
        functools.partial(_out_mlp_kernel, final=final),
        grid=(bsz, nt),
        in_specs=[
            pl.BlockSpec((tm, SSD_WIDTH), t_map),
            pl.BlockSpec((tm, HG_WIDTH), t_map),
            pl.BlockSpec((tm, d), t_map),
            pl.BlockSpec((1, 1, N_MOD * d), lambda b, i: (b, 0, 0)),
            _const_spec(nw.shape),
            _const_spec(w_out.shape),
            _const_spec(w_up.shape),
            _const_spec(w_down.shape),
            _const_spec(fw.shape),
        ],
        out_specs=pl.BlockSpec((tm, d), t_map),
        out_shape=jax.ShapeDtypeStruct(x2.shape, F32),
        compiler_params=pltpu.CompilerParams(
            dimension_semantics=("arbitrary", "arbitrary"), vmem_limit_bytes=VMEM_LIMIT),
        name="out_proj_mlp",
    )(ys, yh, x2, mod_l, nw, w_out, w_up, w_down, fw)


def _pad_lanes(v):
    return jnp.pad(v.astype(F32), ((0, 0), (0, LANES - v.shape[1])))


def kernel(x, c, ada_w, ada_b, norm_mix_w, norm_mlp_w, w_in, conv_w, conv_b, dt_bias, a_log, d_skip,
           ssd_norm_w, hg_lower_bounds, hg_norm_w, w_out, w_up, w_down, final_norm_w):
    bsz, seqlen, d = x.shape
    depth = w_in.shape[0]
    assert d == D_MODEL and seqlen % PROJ_TM == 0 and seqlen % SSD_Q == 0 and seqlen % HG_Q == 0

    mod = _modulation(c.astype(F32), ada_w.astype(F32), ada_b.astype(F32))
    mod = mod.reshape(depth, bsz, 1, N_MOD * d)
    lb_all = _lower_bounds(hg_lower_bounds)

    c_dt = SSD_WIDTH + XBC_WIDTH
    w_main = jnp.concatenate([w_in[:, :, :c_dt], w_in[:, :, c_dt + SSD_HEADS:]], axis=2).astype(BF16)
    w_dt = jnp.pad(w_in[:, :, c_dt:c_dt + SSD_HEADS],
                   ((0, 0), (0, 0), (0, LANES - SSD_HEADS))).astype(BF16)
    w_out_b = w_out.astype(BF16)
    w_up_b = w_up.astype(BF16)
    w_down_b = w_down.astype(BF16)

    heads = np.arange(SSD_WIDTH) // SSD_HEAD_DIM
    e_mat = jnp.asarray((np.arange(LANES)[:, None] == heads[None, :]).astype(np.float32), dtype=BF16)
    level_mats = jnp.asarray(_hg_level_mats(HG_Q), dtype=BF16)
    hg_nw = jnp.tile(hg_norm_w.astype(F32), (1, HG_HEADS))
    fw = final_norm_w.astype(F32).reshape(1, d)

    xc = x.astype(F32).reshape(bsz * seqlen, d)
    for l in range(depth):
        z, xbc, dt, qp, ff, fb, iv, g = _in_proj(
            xc, mod[l], norm_mix_w[l].astype(F32).reshape(1, d), w_main[l], w_dt[l],
            conv_w[l].astype(F32), conv_b[l].astype(F32).reshape(1, XBC_WIDTH),
            bsz=bsz, seqlen=seqlen)
        prm = jnp.concatenate([_pad_lanes(dt_bias[l]), _pad_lanes(a_log[l]),
                               jnp.zeros((4, LANES), F32)], axis=0)
        dexp = jnp.repeat(d_skip[l].astype(F32), SSD_HEAD_DIM).reshape(1, SSD_WIDTH)
        ys = _ssd_mixer(z, xbc, dt, prm, dexp, ssd_norm_w[l].astype(F32).reshape(1, SSD_WIDTH),
                        e_mat, bsz=bsz, seqlen=seqlen)
        yh = _hg_mixer(qp, ff, fb, iv, g, lb_all, l, hg_nw[l].reshape(1, HG_WIDTH), level_mats,
                       bsz=bsz, seqlen=seqlen)
        xc = _out_mlp(ys, yh, xc, mod[l], norm_mlp_w[l].astype(F32).reshape(1, d),
                      w_out_b[l], w_up_b[l], w_down_b[l], fw,
                      bsz=bsz, seqlen=seqlen, final=(l == depth - 1))
    return xc.reshape(bsz, seqlen, d).astype(x.dtype)
```

```python
import functools

import numpy as np
import jax
import jax.numpy as jnp
from jax import lax
from jax.experimental import pallas as pl
from jax.experimental.pallas import tpu as pltpu

F32 = jnp.float32
BF16 = jnp.bfloat16

D_MODEL = 1024
N_MOD = 6
SSD_WIDTH = 1024
SSD_HEAD_DIM = 64
SSD_HEADS = 16
SSD_GROUPS = 2
SSD_STATE = 128
SSD_CONV = 5
XBC_WIDTH = SSD_WIDTH + 2 * SSD_GROUPS * SSD_STATE
HG_WIDTH = 1024
HG_HEADS = 8
HG_DK = 128
HG_DV = 128
D_FF = 4096
EPS = 1e-6
LB_FLOOR = 1e-30

LANES = 128
HALO = 16
VMEM_LIMIT = 56 * 1024 * 1024
NEG = -1e30

PROJ_TM = 512
PROJ_NC = 512
SSD_Q = 128
HG_Q = 128
MIX_SUB = 4

def _sigmoid(v):
    return 1.0 / (1.0 + jnp.exp(-v))


def _silu(v):
    return v * _sigmoid(v)


def _softplus(v):
    return jnp.maximum(v, 0.0) + jnp.log(1.0 + jnp.exp(-jnp.abs(v)))


def _dot(a, b):
    return jnp.dot(a, b, preferred_element_type=F32)


def _dot_nt(a, b):
    return lax.dot_general(a, b, (((1,), (1,)), ((), ())), preferred_element_type=F32)


def _split2(v):
    hi = v.astype(BF16)
    lo = (v - hi.astype(F32)).astype(BF16)
    return hi, lo


def _split3(v):
    hi = v.astype(BF16)
    r = v - hi.astype(F32)
    mid = r.astype(BF16)
    lo = (r - mid.astype(F32)).astype(BF16)
    return hi, mid, lo


def _const_spec(shape):
    nd = len(shape)
    return pl.BlockSpec(shape, lambda *_: (0,) * nd, pipeline_mode=pl.Buffered(1))


def _mod_kernel(c_ref, w_ref, b_ref, o_ref):
    ca = _silu(c_ref[...]).astype(BF16)
    o_ref[0] = _dot(ca, w_ref[0].astype(BF16)) + b_ref[0]


def _modulation(c, ada_w, ada_b):
    depth, d, n = ada_w.shape
    bsz = c.shape[0]
    nb = n // d
    return pl.pallas_call(
        _mod_kernel,
        grid=(depth, nb),
        in_specs=[
            pl.BlockSpec((bsz, d), lambda l, j: (0, 0)),
            pl.BlockSpec((1, d, d), lambda l, j: (l, 0, j)),
            pl.BlockSpec((1, 1, d), lambda l, j: (l, 0, j)),
        ],
        out_specs=pl.BlockSpec((1, bsz, d), lambda l, j: (l, 0, j)),
        out_shape=jax.ShapeDtypeStruct((depth, bsz, n), F32),
        compiler_params=pltpu.CompilerParams(
            dimension_semantics=("arbitrary", "arbitrary"), vmem_limit_bytes=VMEM_LIMIT),
        name="adaln_modulation",
    )(c, ada_w, ada_b.reshape(depth, 1, n))


def _lb_kernel(lb_ref, o_ref):
    depth = lb_ref.shape[1]
    for d in range(2):
        rows = [lb_ref[d, l:l + 1, :] for l in range(depth)]
        m = rows[0]
        for r in rows[1:]:
            m = jnp.maximum(m, r)
        es = [jnp.exp(r - m) for r in rows]
        tot = es[0]
        for e in es[1:]:
            tot = tot + e
        inv = 1.0 / tot
        ps = [e * inv for e in es]
        run = ps[0]
        for l in range(depth):
            if l > 0:
                run = run + ps[l]
            lb = run - ps[0]
            o_ref[l, 2 * d:2 * d + 1, :] = jnp.maximum(lb, LB_FLOOR)
            o_ref[l, 2 * d + 1:2 * d + 2, :] = 1.0 - lb
    o_ref[:, 4:8, :] = jnp.zeros((depth, 4, o_ref.shape[2]), F32)


def _lower_bounds(hg_lower_bounds):
    _, depth, w = hg_lower_bounds.shape
    return pl.pallas_call(
        _lb_kernel,
        out_shape=jax.ShapeDtypeStruct((depth, 8, w), F32),
        name="hgrn2_lower_bounds",
    )(hg_lower_bounds.astype(F32))


def _inproj_kernel(x_ref, xp_ref, xn_ref, mod_ref, nw_ref, w_ref, wdt_ref, cw_ref, cb_ref,
                   z_ref, xbc_ref, dt_ref, q_ref, ff_ref, fb_ref, iv_ref, g_ref,
                   h_scr, xe_scr, *, tm):
    i = pl.program_id(1)
    n_i = pl.num_programs(1)
    d = D_MODEL
    shift = mod_ref[0, :, 0:d]
    scale = mod_ref[0, :, d:2 * d]
    nw = nw_ref[...]

    def norm_mod(xv):
        ms = jnp.mean(xv * xv, axis=-1, keepdims=True)
        y = xv * lax.rsqrt(ms + EPS) * nw
        return (y * (1.0 + scale) + shift).astype(BF16)

    h_scr[0:HALO, :] = norm_mod(xp_ref[...])
    h_scr[HALO:HALO + tm, :] = norm_mod(x_ref[...])
    h_scr[HALO + tm:2 * HALO + tm, :] = norm_mod(xn_ref[...])

    hm = h_scr[HALO:HALO + tm, :]
    nc = PROJ_NC

    def plain(out_ref, col0, width):
        for c in range(0, width, nc):
            acc = _dot(hm, w_ref[:, col0 + c:col0 + c + nc])
            out_ref[:, c:c + nc] = acc.astype(out_ref.dtype)

    plain(z_ref, 0, SSD_WIDTH)
    col = SSD_WIDTH

    he = h_scr[...]
    for c in range(0, XBC_WIDTH, nc):
        xe_scr[...] = _dot(he, w_ref[:, col + c:col + c + nc])

        @pl.when(i == 0)
        def _():
            xe_scr[0:HALO, :] = jnp.zeros((HALO, nc), F32)

        @pl.when(i == n_i - 1)
        def _():
            xe_scr[HALO + tm:2 * HALO + tm, :] = jnp.zeros((HALO, nc), F32)

        acc = jnp.zeros((tm, nc), F32) + cb_ref[:, c:c + nc]
        for k in range(SSD_CONV):
            off = HALO - SSD_CONV // 2 + k
            acc = acc + cw_ref[k:k + 1, c:c + nc] * xe_scr[off:off + tm, :]
        xbc_ref[:, c:c + nc] = _silu(acc).astype(xbc_ref.dtype)
    col += XBC_WIDTH

    for out_ref in (q_ref, ff_ref, fb_ref, iv_ref, g_ref):
        plain(out_ref, col, HG_WIDTH)
        col += HG_WIDTH

    dt_ref[...] = _dot(hm, wdt_ref[...])


def _in_proj(x2, mod_l, nw, w_main, w_dt, conv_w, conv_b, *, bsz, seqlen):
    tm = PROJ_TM
    nt = seqlen // tm
    hb = tm // HALO
    n_halo_blocks = bsz * seqlen // HALO
    tok = bsz * seqlen
    d = D_MODEL

    def x_map(b, i):
        return (b * nt + i, 0)

    def xp_map(b, i):
        return (jnp.maximum((b * nt + i) * hb - 1, 0), 0)

    def xn_map(b, i):
        return (jnp.minimum((b * nt + i + 1) * hb, n_halo_blocks - 1), 0)

    def tile(width, dtype):
        return pl.BlockSpec((tm, width), x_map), jax.ShapeDtypeStruct((tok, width), dtype)

    outs = [tile(SSD_WIDTH, BF16), tile(XBC_WIDTH, BF16), tile(LANES, F32)] + [tile(HG_WIDTH, BF16)] * 5
    return pl.pallas_call(
        functools.partial(_inproj_kernel, tm=tm),
        grid=(bsz, nt),
        in_specs=[
            pl.BlockSpec((tm, d), x_map),
            pl.BlockSpec((HALO, d), xp_map),
            pl.BlockSpec((HALO, d), xn_map),
            pl.BlockSpec((1, 1, N_MOD * d), lambda b, i: (b, 0, 0)),
            _const_spec((1, d)),
            _const_spec(w_main.shape),
            _const_spec(w_dt.shape),
            _const_spec(conv_w.shape),
            _const_spec(conv_b.shape),
        ],
        out_specs=[o[0] for o in outs],
        out_shape=[o[1] for o in outs],
        scratch_shapes=[
            pltpu.VMEM((tm + 2 * HALO, d), BF16),
            pltpu.VMEM((tm + 2 * HALO, PROJ_NC), F32),
        ],
        compiler_params=pltpu.CompilerParams(
            dimension_semantics=("arbitrary", "arbitrary"), vmem_limit_bytes=VMEM_LIMIT),
        name="in_proj",
    )(x2, x2, x2, mod_l, nw, w_main, w_dt, conv_w, conv_b)


def _ssd_kernel(z_ref, xbc_ref, dt_ref, prm_ref, dexp_ref, nw_ref, e_ref, o_ref,
                yf_scr, st_scr, *, n_steps, n_sub, q):
    j = pl.program_id(1)
    hpg = SSD_HEADS // SSD_GROUPS
    gw = hpg * SSD_HEAD_DIM

    @pl.when((j == 0) | (j == n_steps))
    def _():
        st_scr[...] = jnp.zeros(st_scr.shape, F32)

    def chunk(direction, i):
        blk = j if direction == 0 else 2 * n_steps - 1 - j
        r0 = pl.multiple_of((i if direction == 0 else n_sub - 1 - i) * q, q)
        row0 = pl.multiple_of(blk * (n_sub * q) + r0, q)
        xbc = xbc_ref[pl.ds(r0, q), :]
        x_bf = xbc[:, 0:SSD_WIDTH]
        x_f = x_bf.astype(F32)
        bias = prm_ref[direction:direction + 1, :]
        a_neg = -jnp.exp(prm_ref[2 + direction:3 + direction, :])
        dtv = _softplus(dt_ref[pl.ds(r0, q), :] + bias)
        a = dtv * a_neg

        rows = lax.broadcasted_iota(jnp.int32, (q, q), 0)
        cols = lax.broadcasted_iota(jnp.int32, (q, q), 1)
        tri = (cols <= rows) if direction == 0 else (cols >= rows)
        tri_bf = jnp.where(tri, 1.0, 0.0).astype(BF16)
        a1, a2, a3 = _split3(a)
        cum = _dot(tri_bf, a1) + _dot(tri_bf, a2) + _dot(tri_bf, a3)
        end = q - 1 if direction == 0 else 0
        cum_end = cum[end:end + 1, :]
        ecum = jnp.exp(cum)
        wst = jnp.exp(cum_end - cum) * dtv
        cum_t = cum.T
        dt_t = dtv.T

        e_mat = e_ref[...]
        e1, e2 = _split2(ecum)
        ecum_x = _dot(e1, e_mat) + _dot(e2, e_mat)
        wst_x = _dot(wst.astype(BF16), e_mat)
        xw = (x_f * wst_x).astype(BF16)

        lane = lax.broadcasted_iota(jnp.int32, (q, LANES), 1)
        low_half = lane < SSD_HEAD_DIM

        y_parts = []
        for g in range(SSD_GROUPS):
            b_bf = xbc[:, SSD_WIDTH + g * SSD_STATE:SSD_WIDTH + (g + 1) * SSD_STATE]
            c0 = SSD_WIDTH + SSD_GROUPS * SSD_STATE + g * SSD_STATE
            c_bf = xbc[:, c0:c0 + SSD_STATE]
            scores = _dot_nt(c_bf, b_bf)

            st = st_scr[g]
            y_inter = _dot(c_bf, st.astype(BF16)) * ecum_x[:, g * gw:(g + 1) * gw]
            b_t = b_bf.astype(F32).T.astype(BF16)
            dec = ecum_x[end:end + 1, g * gw:(g + 1) * gw]
            st_scr[g] = st * dec + _dot(b_t, xw[:, g * gw:(g + 1) * gw])

            for p in range(hpg // 2):
                ws = []
                for h in (g * hpg + 2 * p, g * hpg + 2 * p + 1):
                    seg = cum[:, h:h + 1] - cum_t[h:h + 1, :]
                    decay = jnp.exp(jnp.where(tri, seg, NEG))
                    ws.append((scores * decay * dt_t[h:h + 1, :]).astype(BF16))
                w_pair = jnp.concatenate(ws, axis=1)
                l0 = g * gw + p * LANES
                xp = x_bf[:, l0:l0 + LANES]
                zero = jnp.zeros_like(xp)
                x_bd = jnp.concatenate([jnp.where(low_half, xp, zero),
                                        jnp.where(low_half, zero, xp)], axis=0)
                y_parts.append(_dot(w_pair, x_bd) + y_inter[:, p * LANES:(p + 1) * LANES])
        y = jnp.concatenate(y_parts, axis=1)

        if direction == 0:
            yf_scr[pl.ds(row0, q), :] = y + x_f * dexp_ref[...]
        else:
            y = y + yf_scr[pl.ds(row0, q), :]
            y = y * _silu(z_ref[pl.ds(r0, q), :].astype(F32))
            ms = jnp.mean(y * y, axis=-1, keepdims=True)
            o_ref[pl.ds(r0, q), :] = (y * lax.rsqrt(ms + EPS) * nw_ref[...]).astype(o_ref.dtype)

    def run(direction):
        def body(i, carry):
            chunk(direction, i)
            return carry
        lax.fori_loop(0, n_sub, body, 0)

    @pl.when(j < n_steps)
    def _():
        run(0)

    @pl.when(j >= n_steps)
    def _():
        run(1)


def _two_phase_maps(n_steps):
    def both(b, j):
        return (b * n_steps + jnp.where(j < n_steps, j, 2 * n_steps - 1 - j), 0)

    def first(b, j):
        return (b * n_steps + jnp.where(j < n_steps, j, n_steps - 1), 0)

    def second(b, j):
        return (b * n_steps + jnp.where(j < n_steps, n_steps - 1, 2 * n_steps - 1 - j), 0)

    return both, first, second


def _ssd_mixer(z, xbc, dt, prm, dexp, nw, e_mat, *, bsz, seqlen):
    q = SSD_Q
    rows = MIX_SUB * q
    n_steps = seqlen // rows
    tok = bsz * seqlen
    both, _, second = _two_phase_maps(n_steps)

    return pl.pallas_call(
        functools.partial(_ssd_kernel, n_steps=n_steps, n_sub=MIX_SUB, q=q),
        grid=(bsz, 2 * n_steps),
        in_specs=[
            pl.BlockSpec((rows, SSD_WIDTH), second),
            pl.BlockSpec((rows, XBC_WIDTH), both),
            pl.BlockSpec((rows, LANES), both),
            _const_spec(prm.shape),
            _const_spec(dexp.shape),
            _const_spec(nw.shape),
            _const_spec(e_mat.shape),
        ],
        out_specs=pl.BlockSpec((rows, SSD_WIDTH), second),
        out_shape=jax.ShapeDtypeStruct((tok, SSD_WIDTH), BF16),
        scratch_shapes=[
            pltpu.VMEM((seqlen, SSD_WIDTH), F32),
            pltpu.VMEM((SSD_GROUPS, SSD_STATE, SSD_WIDTH // SSD_GROUPS), F32),
        ],
        compiler_params=pltpu.CompilerParams(
            dimension_semantics=("arbitrary", "arbitrary"), vmem_limit_bytes=VMEM_LIMIT),
        name="ssd_mixer",
    )(z, xbc, dt, prm, dexp, nw, e_mat)


HG_MXU_LEVELS = (1, 2)


def _hg_level_mats(q):
    mats = np.zeros((2, 1 + len(HG_MXU_LEVELS), q, q), np.float32)
    t = np.arange(q)[:, None]
    r = np.arange(q)[None, :]
    mats[0, 0] = r <= t
    mats[1, 0] = r >= t
    for i, lev in enumerate(HG_MXU_LEVELS):
        same = (t >> (lev + 1)) == (r >> (lev + 1))
        t_up = ((t >> lev) & 1) == 1
        r_up = ((r >> lev) & 1) == 1
        mats[0, 1 + i] = same & ((t_up & r_up & (r <= t)) | (~t_up & ~r_up & (r > t)))
        mats[1, 1 + i] = same & ((~t_up & ~r_up & (r >= t)) | (t_up & r_up & (r < t)))
    return np.concatenate([mats, mats], axis=3).reshape(2, -1, 2 * q)


def _hg_pair_masks(q):
    n_lev = int(np.log2(q))
    out = np.zeros((2, n_lev + 1, q, q), np.float32)
    t = np.arange(q)[:, None]
    s = np.arange(q)[None, :]
    for lev in range(n_lev):
        same = (t >> (lev + 1)) == (s >> (lev + 1))
        t_up = ((t >> lev) & 1) == 1
        s_up = ((s >> lev) & 1) == 1
        out[0, lev] = same & t_up & ~s_up
        out[1, lev] = same & ~t_up & s_up
    out[:, n_lev] = t == s
    return out


def _hg_kernel(q_ref, ff_ref, fb_ref, iv_ref, g_ref, lb_ref, nw_ref, m_ref, pm_ref, o_ref,
               yf_scr, st_scr, stt_scr, *, n_steps, n_sub, q):
    j = pl.program_id(1)
    n_lev = int(np.log2(q))
    tile = 8

    @pl.when((j == 0) | (j == n_steps))
    def _():
        st_scr[...] = jnp.zeros(st_scr.shape, F32)
        stt_scr[...] = jnp.zeros(stt_scr.shape, BF16)

    def chunk(direction, i):
        fwd = direction == 0
        blk = j if fwd else 2 * n_steps - 1 - j
        r0 = pl.multiple_of((i if fwd else n_sub - 1 - i) * q, q)
        row0 = pl.multiple_of(blk * (n_sub * q) + r0, q)
        rs = pl.ds(r0, q)
        f_ref = ff_ref if fwd else fb_ref
        q_bf = q_ref[rs, :]
        v_bf = iv_ref[rs, :]
        fh = f_ref[rs, :].astype(F32)
        lbf = lb_ref[0, 2 * direction:2 * direction + 1, :]
        oml = lb_ref[0, 2 * direction + 1:2 * direction + 2, :]
        e = jnp.exp(-jnp.abs(fh))
        r = 1.0 / (1.0 + e)
        er = e * r
        pos = fh >= 0.0
        fval = lbf + oml * jnp.where(pos, r, er)
        k_f = oml * jnp.where(pos, er, r)
        l_hi, l_lo = _split2(jnp.log2(fval))
        l_cat = jnp.concatenate([l_hi, l_lo], axis=0)
        sums = _dot(m_ref[direction], l_cat)
        cum = sums[0:q]

        def pick_rows(lev, a, b):
            sub = lax.broadcasted_iota(jnp.int32, (1, tile, HG_DK), 1)
            up = ((sub >> lev) & 1) == 1
            is_query = up if fwd else jnp.logical_not(up)
            shp = (q // tile, tile, HG_DK)
            return jnp.where(is_query, a.reshape(shp), b.reshape(shp)).reshape(q, HG_DK)

        end = q - 1 if fwd else 0
        y_parts = []
        for h in range(HG_HEADS):
            sl = slice(h * HG_DK, (h + 1) * HG_DK)
            q_h = q_bf[:, sl].astype(F32)
            k_h = k_f[:, sl]
            cum_h = cum[:, sl]
            v_h = v_bf[:, sl]

            attn = pm_ref[direction, n_lev] * jnp.sum(q_h * k_h, axis=-1, keepdims=True)
            for lev in range(n_lev):
                if lev == 0:
                    x_l = pick_rows(0, q_h * fval[:, sl], k_h)
                elif lev in HG_MXU_LEVELS:
                    i_l = 1 + HG_MXU_LEVELS.index(lev)
                    x_l = pick_rows(lev, q_h, k_h) * jnp.exp2(sums[i_l * q:(i_l + 1) * q, sl])
                else:
                    half = 1 << lev
                    slabs = []
                    for b0 in range(0, q, 2 * half):
                        lo = slice(b0, b0 + half)
                        hi = slice(b0 + half, b0 + 2 * half)
                        if fwd:
                            ref = cum_h[b0 + half - 1:b0 + half, :]
                            slabs.append(k_h[lo] * jnp.exp2(ref - cum_h[lo]))
                            slabs.append(q_h[hi] * jnp.exp2(cum_h[hi] - ref))
                        else:
                            ref = cum_h[b0 + half:b0 + half + 1, :]
                            slabs.append(q_h[lo] * jnp.exp2(cum_h[lo] - ref))
                            slabs.append(k_h[hi] * jnp.exp2(ref - cum_h[hi]))
                    x_l = jnp.concatenate(slabs, axis=0)
                x_l = x_l.astype(BF16)
                attn = attn + pm_ref[direction, lev] * _dot_nt(x_l, x_l)

            eq = jnp.exp2(cum_h)
            xq = (q_h * eq).astype(BF16)
            xk = (k_h * jnp.exp2(cum_h[end:end + 1, :] - cum_h)).astype(BF16)
            lhs = jnp.concatenate([attn.astype(BF16), xq], axis=1)
            rhs = jnp.concatenate([v_h, stt_scr[h]], axis=0)
            y_parts.append(_dot(lhs, rhs))
            v_t = v_h.astype(F32).T.astype(BF16)
            st_new = st_scr[h] * eq[end:end + 1, :] + _dot(v_t, xk)
            st_scr[h] = st_new
            stt_scr[h] = st_new.T.astype(BF16)
        y = jnp.concatenate(y_parts, axis=1)

        if fwd:
            yf_scr[pl.ds(row0, q), :] = y
        else:
            y = y + yf_scr[pl.ds(row0, q), :]
            gate = _silu(g_ref[rs, :].astype(F32))
            nw = nw_ref[...]
            for h in range(HG_HEADS):
                sl = slice(h * HG_DV, (h + 1) * HG_DV)
                y_h = y[:, sl]
                ms = jnp.mean(y_h * y_h, axis=-1, keepdims=True)
                o_ref[rs, sl] = (y_h * lax.rsqrt(ms + EPS) * nw[:, sl] * gate[:, sl]).astype(o_ref.dtype)

    def run(direction):
        def body(i, carry):
            chunk(direction, i)
            return carry
        lax.fori_loop(0, n_sub, body, 0)

    @pl.when(j < n_steps)
    def _():
        run(0)

    @pl.when(j >= n_steps)
    def _():
        run(1)


def _hg_mixer(qp, ff, fb, iv, g, lb_all, layer, nw, level_mats, pair_masks, *, bsz, seqlen):
    q = HG_Q
    rows = MIX_SUB * q
    n_steps = seqlen // rows
    tok = bsz * seqlen
    both, first, second = _two_phase_maps(n_steps)

    blk = lambda m: pl.BlockSpec((rows, HG_WIDTH), m)
    return pl.pallas_call(
        functools.partial(_hg_kernel, n_steps=n_steps, n_sub=MIX_SUB, q=q),
        grid=(bsz, 2 * n_steps),
        in_specs=[
            blk(both), blk(first), blk(second), blk(both), blk(second),
            pl.BlockSpec((1, 8, HG_WIDTH), lambda b, j: (layer, 0, 0), pipeline_mode=pl.Buffered(1)),
            _const_spec(nw.shape),
            _const_spec(level_mats.shape),
            _const_spec(pair_masks.shape),
        ],
        out_specs=blk(second),
        out_shape=jax.ShapeDtypeStruct((tok, HG_WIDTH), BF16),
        scratch_shapes=[
            pltpu.VMEM((seqlen, HG_WIDTH), F32),
            pltpu.VMEM((HG_HEADS, HG_DV, HG_DK), F32),
            pltpu.VMEM((HG_HEADS, HG_DK, HG_DV), BF16),
        ],
        compiler_params=pltpu.CompilerParams(
            dimension_semantics=("arbitrary", "arbitrary"), vmem_limit_bytes=VMEM_LIMIT),
        name="hgrn2_mixer",
    )(qp, ff, fb, iv, g, lb_all, nw, level_mats, pair_masks)


def _out_mlp_kernel(ys_ref, yh_ref, x_ref, mod_ref, nw_ref, wo_ref, wu_ref, wd_ref, fw_ref, o_ref,
                    *, final):
    d = D_MODEL
    gate1 = mod_ref[0, :, 2 * d:3 * d]
    shift2 = mod_ref[0, :, 3 * d:4 * d]
    scale2 = mod_ref[0, :, 4 * d:5 * d]
    gate2 = mod_ref[0, :, 5 * d:6 * d]
    y = _dot(ys_ref[...], wo_ref[0:SSD_WIDTH, :]) + _dot(yh_ref[...], wo_ref[SSD_WIDTH:, :])
    x1 = x_ref[...] + gate1 * y
    ms = jnp.mean(x1 * x1, axis=-1, keepdims=True)
    h = ((x1 * lax.rsqrt(ms + EPS) * nw_ref[...]) * (1.0 + scale2) + shift2).astype(BF16)
    acc = jnp.zeros(x1.shape, F32)
    nc = 2 * PROJ_NC
    for c in range(0, D_FF, nc):
        u = jnp.maximum(_dot(h, wu_ref[:, c:c + nc]), 0.0)
        acc = acc + _dot((u * u).astype(BF16), wd_ref[c:c + nc, :])
    x2 = x1 + gate2 * acc
    if final:
        ms2 = jnp.mean(x2 * x2, axis=-1, keepdims=True)
        x2 = x2 * lax.rsqrt(ms2 + EPS) * fw_ref[...]
    o_ref[...] = x2


def _out_mlp(ys, yh, x2, mod_l, nw, w_out, w_up, w_down, fw, *, bsz, seqlen, final):
    tm = PROJ_TM
    nt = seqlen // tm
    d = D_MODEL

    def t_map(b, i):
        return (b * nt + i, 0)

    return pl.pallas_call(
        functools.partial(_out_mlp_kernel, final=final),
        grid=(bsz, nt),
        in_specs=[
            pl.BlockSpec((tm, SSD_WIDTH), t_map),
            pl.BlockSpec((tm, HG_WIDTH), t_map),
            pl.BlockSpec((tm, d), t_map),
            pl.BlockSpec((1, 1, N_MOD * d), lambda b, i: (b, 0, 0)),
            _const_spec(nw.shape),
            _const_spec(w_out.shape),
            _const_spec(w_up.shape),
            _const_spec(w_down.shape),
            _const_spec(fw.shape),
        ],
        out_specs=pl.BlockSpec((tm, d), t_map),
        out_shape=jax.ShapeDtypeStruct(x2.shape, F32),
        compiler_params=pltpu.CompilerParams(
            dimension_semantics=("arbitrary", "arbitrary"), vmem_limit_bytes=VMEM_LIMIT),
        name="out_proj_mlp",
    )(ys, yh, x2, mod_l, nw, w_out, w_up, w_down, fw)


def _pad_lanes(v):
    return jnp.pad(v.astype(F32), ((0, 0), (0, LANES - v.shape[1])))


def kernel(x, c, ada_w, ada_b, norm_mix_w, norm_mlp_w, w_in, conv_w, conv_b, dt_bias, a_log, d_skip,
           ssd_norm_w, hg_lower_bounds, hg_norm_w, w_out, w_up, w_down, final_norm_w):
    bsz, seqlen, d = x.shape
    depth = w_in.shape[0]
    assert d == D_MODEL and seqlen % PROJ_TM == 0 and seqlen % SSD_Q == 0 and seqlen % HG_Q == 0

    mod = _modulation(c.astype(F32), ada_w.astype(F32), ada_b.astype(F32))
    mod = mod.reshape(depth, bsz, 1, N_MOD * d)
    lb_all = _lower_bounds(hg_lower_bounds)

    c_dt = SSD_WIDTH + XBC_WIDTH
    w_main = jnp.concatenate([w_in[:, :, :c_dt], w_in[:, :, c_dt + SSD_HEADS:]], axis=2).astype(BF16)
    w_dt = jnp.pad(w_in[:, :, c_dt:c_dt + SSD_HEADS],
                   ((0, 0), (0, 0), (0, LANES - SSD_HEADS))).astype(BF16)
    w_out_b = w_out.astype(BF16)
    w_up_b = w_up.astype(BF16)
    w_down_b = w_down.astype(BF16)

    heads = np.arange(SSD_WIDTH) // SSD_HEAD_DIM
    e_mat = jnp.asarray((np.arange(LANES)[:, None] == heads[None, :]).astype(np.float32), dtype=BF16)
    level_mats = jnp.asarray(_hg_level_mats(HG_Q), dtype=BF16)
    pair_masks = jnp.asarray(_hg_pair_masks(HG_Q), dtype=F32)
    hg_nw = jnp.tile(hg_norm_w.astype(F32), (1, HG_HEADS))
    fw = final_norm_w.astype(F32).reshape(1, d)

    xc = x.astype(F32).reshape(bsz * seqlen, d)
    for l in range(depth):
        z, xbc, dt, qp, ff, fb, iv, g = _in_proj(
            xc, mod[l], norm_mix_w[l].astype(F32).reshape(1, d), w_main[l], w_dt[l],
            conv_w[l].astype(F32), conv_b[l].astype(F32).reshape(1, XBC_WIDTH),
            bsz=bsz, seqlen=seqlen)
        prm = jnp.concatenate([_pad_lanes(dt_bias[l]), _pad_lanes(a_log[l]),
                               jnp.zeros((4, LANES), F32)], axis=0)
        dexp = jnp.repeat(d_skip[l].astype(F32), SSD_HEAD_DIM).reshape(1, SSD_WIDTH)
        ys = _ssd_mixer(z, xbc, dt, prm, dexp, ssd_norm_w[l].astype(F32).reshape(1, SSD_WIDTH),
                        e_mat, bsz=bsz, seqlen=seqlen)
        yh = _hg_mixer(qp, ff, fb, iv, g, lb_all, l, hg_nw[l].reshape(1, HG_WIDTH), level_mats, pair_masks,
                       bsz=bsz, seqlen=seqlen)
        xc = _out_mlp(ys, yh, xc, mod[l], norm_mlp_w[l].astype(F32).reshape(1, d),
                      w_out_b[l], w_up_b[l], w_down_b[l], fw,
                      bsz=bsz, seqlen=seqlen, final=(l == depth - 1))
    return xc.reshape(bsz, seqlen, d).astype(x.dtype)
```

```python
import functools

import numpy as np
import jax
import jax.numpy as jnp
from jax import lax
from jax.experimental import pallas as pl
from jax.experimental.pallas import tpu as pltpu

F32 = jnp.float32
BF16 = jnp.bfloat16

D_MODEL = 1024
N_MOD = 6
SSD_WIDTH = 1024
SSD_HEAD_DIM = 64
SSD_HEADS = 16
SSD_GROUPS = 2
SSD_STATE = 128
SSD_CONV = 5
XBC_WIDTH = SSD_WIDTH + 2 * SSD_GROUPS * SSD_STATE
HG_WIDTH = 1024
HG_HEADS = 8
HG_DK = 128
HG_DV = 128
D_FF = 4096
EPS = 1e-6
LB_FLOOR = 1e-30

LANES = 128
HALO = 16
VMEM_LIMIT = 56 * 1024 * 1024
NEG = -1e30
LOG2E = 1.4426950408889634

PROJ_TM = 512
PROJ_NC = 512
SSD_Q = 128
HG_Q = 128
MIX_SUB = 4

def _sigmoid(v):
    return 1.0 / (1.0 + jnp.exp(-v))


def _silu(v):
    return v * _sigmoid(v)


def _softplus(v):
    return jnp.maximum(v, 0.0) + jnp.log(1.0 + jnp.exp(-jnp.abs(v)))


def _dot(a, b):
    return jnp.dot(a, b, preferred_element_type=F32)


def _dot_nt(a, b):
    return lax.dot_general(a, b, (((1,), (1,)), ((), ())), preferred_element_type=F32)


def _split2(v):
    hi = v.astype(BF16)
    lo = (v - hi.astype(F32)).astype(BF16)
    return hi, lo


def _split3(v):
    hi = v.astype(BF16)
    r = v - hi.astype(F32)
    mid = r.astype(BF16)
    lo = (r - mid.astype(F32)).astype(BF16)
    return hi, mid, lo


def _const_spec(shape):
    nd = len(shape)
    return pl.BlockSpec(shape, lambda *_: (0,) * nd, pipeline_mode=pl.Buffered(1))


def _mod_kernel(c_ref, w_ref, b_ref, o_ref):
    ca = _silu(c_ref[...]).astype(BF16)
    o_ref[0] = _dot(ca, w_ref[0].astype(BF16)) + b_ref[0]


def _modulation(c, ada_w, ada_b):
    depth, d, n = ada_w.shape
    bsz = c.shape[0]
    nb = n // d
    return pl.pallas_call(
        _mod_kernel,
        grid=(depth, nb),
        in_specs=[
            pl.BlockSpec((bsz, d), lambda l, j: (0, 0)),
            pl.BlockSpec((1, d, d), lambda l, j: (l, 0, j)),
            pl.BlockSpec((1, 1, d), lambda l, j: (l, 0, j)),
        ],
        out_specs=pl.BlockSpec((1, bsz, d), lambda l, j: (l, 0, j)),
        out_shape=jax.ShapeDtypeStruct((depth, bsz, n), F32),
        compiler_params=pltpu.CompilerParams(
            dimension_semantics=("arbitrary", "arbitrary"), vmem_limit_bytes=VMEM_LIMIT),
        name="adaln_modulation",
    )(c, ada_w, ada_b.reshape(depth, 1, n))


def _lb_kernel(lb_ref, o_ref):
    depth = lb_ref.shape[1]
    for d in range(2):
        rows = [lb_ref[d, l:l + 1, :] for l in range(depth)]
        m = rows[0]
        for r in rows[1:]:
            m = jnp.maximum(m, r)
        es = [jnp.exp(r - m) for r in rows]
        tot = es[0]
        for e in es[1:]:
            tot = tot + e
        inv = 1.0 / tot
        ps = [e * inv for e in es]
        run = ps[0]
        for l in range(depth):
            if l > 0:
                run = run + ps[l]
            lb = run - ps[0]
            o_ref[l, 2 * d:2 * d + 1, :] = jnp.maximum(lb, LB_FLOOR)
            o_ref[l, 2 * d + 1:2 * d + 2, :] = 1.0 - lb
    o_ref[:, 4:8, :] = jnp.zeros((depth, 4, o_ref.shape[2]), F32)


def _lower_bounds(hg_lower_bounds):
    _, depth, w = hg_lower_bounds.shape
    return pl.pallas_call(
        _lb_kernel,
        out_shape=jax.ShapeDtypeStruct((depth, 8, w), F32),
        name="hgrn2_lower_bounds",
    )(hg_lower_bounds.astype(F32))


def _inproj_kernel(x_ref, xp_ref, xn_ref, mod_ref, nw_ref, w_ref, wdt_ref, cw_ref, cb_ref,
                   z_ref, xbc_ref, dt_ref, q_ref, ff_ref, fb_ref, iv_ref, g_ref,
                   h_scr, xe_scr, *, tm):
    i = pl.program_id(1)
    n_i = pl.num_programs(1)
    d = D_MODEL
    shift = mod_ref[0, :, 0:d]
    scale = mod_ref[0, :, d:2 * d]
    nw = nw_ref[...]

    def norm_mod(xv):
        ms = jnp.mean(xv * xv, axis=-1, keepdims=True)
        y = xv * lax.rsqrt(ms + EPS) * nw
        return (y * (1.0 + scale) + shift).astype(BF16)

    keep_prev = jnp.where(i > 0, 1.0, 0.0).astype(BF16)
    keep_next = jnp.where(i < n_i - 1, 1.0, 0.0).astype(BF16)
    h_scr[0:HALO, :] = norm_mod(xp_ref[...]) * keep_prev
    h_scr[HALO:HALO + tm, :] = norm_mod(x_ref[...])
    h_scr[HALO + tm:2 * HALO + tm, :] = norm_mod(xn_ref[...]) * keep_next

    hm = h_scr[HALO:HALO + tm, :]
    nc = PROJ_NC

    he = h_scr[...]

    plain = [(z_ref, c, c) for c in range(0, SSD_WIDTH, nc)]
    col = SSD_WIDTH + XBC_WIDTH
    for out_ref in (q_ref, ff_ref, fb_ref, iv_ref, g_ref):
        plain += [(out_ref, c, col + c) for c in range(0, HG_WIDTH, nc)]
        col += HG_WIDTH

    def run_plain(n):
        for _ in range(n):
            out_ref, oc, wc = plain.pop(0)
            out_ref[:, oc:oc + nc] = _dot(hm, w_ref[:, wc:wc + nc]).astype(out_ref.dtype)

    n_xbc = XBC_WIDTH // nc
    for g in range(n_xbc):
        c = g * nc
        xe = xe_scr.at[g]
        xe[...] = _dot(he, w_ref[:, SSD_WIDTH + c:SSD_WIDTH + c + nc])
        run_plain(len(plain) // (n_xbc - g + 1))
        acc = jnp.zeros((tm, nc), F32) + cb_ref[:, c:c + nc]
        for k in range(SSD_CONV):
            off = HALO - SSD_CONV // 2 + k
            acc = acc + cw_ref[k:k + 1, c:c + nc] * xe[off:off + tm, :]
        xbc_ref[:, c:c + nc] = _silu(acc).astype(xbc_ref.dtype)
    run_plain(len(plain))

    dt_ref[...] = _dot(hm, wdt_ref[...])


def _in_proj(x2, mod_l, nw, w_main, w_dt, conv_w, conv_b, *, bsz, seqlen):
    tm = PROJ_TM
    nt = seqlen // tm
    hb = tm // HALO
    n_halo_blocks = bsz * seqlen // HALO
    tok = bsz * seqlen
    d = D_MODEL

    def x_map(b, i):
        return (b * nt + i, 0)

    def xp_map(b, i):
        return (jnp.maximum((b * nt + i) * hb - 1, 0), 0)

    def xn_map(b, i):
        return (jnp.minimum((b * nt + i + 1) * hb, n_halo_blocks - 1), 0)

    def tile(width, dtype):
        return pl.BlockSpec((tm, width), x_map), jax.ShapeDtypeStruct((tok, width), dtype)

    outs = [tile(SSD_WIDTH, BF16), tile(XBC_WIDTH, BF16), tile(LANES, F32)] + [tile(HG_WIDTH, BF16)] * 5
    return pl.pallas_call(
        functools.partial(_inproj_kernel, tm=tm),
        grid=(bsz, nt),
        in_specs=[
            pl.BlockSpec((tm, d), x_map),
            pl.BlockSpec((HALO, d), xp_map),
            pl.BlockSpec((HALO, d), xn_map),
            pl.BlockSpec((1, 1, N_MOD * d), lambda b, i: (b, 0, 0)),
            _const_spec((1, d)),
            _const_spec(w_main.shape),
            _const_spec(w_dt.shape),
            _const_spec(conv_w.shape),
            _const_spec(conv_b.shape),
        ],
        out_specs=[o[0] for o in outs],
        out_shape=[o[1] for o in outs],
        scratch_shapes=[
            pltpu.VMEM((tm + 2 * HALO, d), BF16),
            pltpu.VMEM((XBC_WIDTH // PROJ_NC, tm + 2 * HALO, PROJ_NC), F32),
        ],
        compiler_params=pltpu.CompilerParams(
            dimension_semantics=("arbitrary", "arbitrary"), vmem_limit_bytes=VMEM_LIMIT),
        name="in_proj",
    )(x2, x2, x2, mod_l, nw, w_main, w_dt, conv_w, conv_b)


def _ssd_kernel(z_ref, xbc_ref, dt_ref, prm_ref, dexp_ref, nw_ref, e_ref, o_ref,
                yf_scr, st_scr, *, n_steps, n_sub, q):
    j = pl.program_id(1)
    hpg = SSD_HEADS // SSD_GROUPS
    gw = hpg * SSD_HEAD_DIM

    @pl.when((j == 0) | (j == n_steps))
    def _():
        st_scr[...] = jnp.zeros(st_scr.shape, F32)

    def chunk(direction, i):
        blk = j if direction == 0 else 2 * n_steps - 1 - j
        r0 = pl.multiple_of((i if direction == 0 else n_sub - 1 - i) * q, q)
        row0 = pl.multiple_of(blk * (n_sub * q) + r0, q)
        xbc = xbc_ref[pl.ds(r0, q), :]
        x_bf = xbc[:, 0:SSD_WIDTH]
        x_f = x_bf.astype(F32)
        bias = prm_ref[direction:direction + 1, :]
        a_neg = -jnp.exp(prm_ref[2 + direction:3 + direction, :]) * LOG2E
        dtv = _softplus(dt_ref[pl.ds(r0, q), :] + bias)
        a = dtv * a_neg

        rows = lax.broadcasted_iota(jnp.int32, (q, q), 0)
        cols = lax.broadcasted_iota(jnp.int32, (q, q), 1)
        tri = (cols <= rows) if direction == 0 else (cols >= rows)
        tri_bf = jnp.where(tri, 1.0, 0.0).astype(BF16)
        cum = _dot(jnp.concatenate([tri_bf] * 3, axis=1), jnp.concatenate(_split3(a), axis=0))
        end = q - 1 if direction == 0 else 0
        cum_end = cum[end:end + 1, :]
        ecum = jnp.exp2(cum)
        wst = jnp.exp2(cum_end - cum) * dtv
        cum_t = cum.T
        dt_t = dtv.T

        e_mat = e_ref[...]
        ecum_x = _dot(jnp.concatenate(_split2(ecum), axis=1), jnp.concatenate([e_mat, e_mat], axis=0))
        wst_x = _dot(wst.astype(BF16), e_mat)
        xw = (x_f * wst_x).astype(BF16)

        lane = lax.broadcasted_iota(jnp.int32, (q, LANES), 1)
        low_half = lane < SSD_HEAD_DIM

        y_parts = []
        for g in range(SSD_GROUPS):
            b_bf = xbc[:, SSD_WIDTH + g * SSD_STATE:SSD_WIDTH + (g + 1) * SSD_STATE]
            c0 = SSD_WIDTH + SSD_GROUPS * SSD_STATE + g * SSD_STATE
            c_bf = xbc[:, c0:c0 + SSD_STATE]
            b_t = b_bf.astype(F32).T.astype(BF16)
            scores = _dot(c_bf, b_t)

            st = st_scr[g]
            y_inter = _dot(c_bf, st.astype(BF16)) * ecum_x[:, g * gw:(g + 1) * gw]
            dec = ecum_x[end:end + 1, g * gw:(g + 1) * gw]
            st_scr[g] = st * dec + _dot(b_t, xw[:, g * gw:(g + 1) * gw])

            for p in range(hpg // 2):
                ws = []
                for h in (g * hpg + 2 * p, g * hpg + 2 * p + 1):
                    seg = cum[:, h:h + 1] - cum_t[h:h + 1, :]
                    decay = jnp.exp2(jnp.where(tri, seg, NEG))
                    ws.append((scores * decay * dt_t[h:h + 1, :]).astype(BF16))
                w_pair = jnp.concatenate(ws, axis=1)
                l0 = g * gw + p * LANES
                xp = x_bf[:, l0:l0 + LANES]
                zero = jnp.zeros_like(xp)
                x_bd = jnp.concatenate([jnp.where(low_half, xp, zero),
                                        jnp.where(low_half, zero, xp)], axis=0)
                y_parts.append(_dot(w_pair, x_bd) + y_inter[:, p * LANES:(p + 1) * LANES])
        y = jnp.concatenate(y_parts, axis=1)

        if direction == 0:
            yf_scr[pl.ds(row0, q), :] = y + x_f * dexp_ref[...]
        else:
            y = y + yf_scr[pl.ds(row0, q), :]
            y = y * _silu(z_ref[pl.ds(r0, q), :].astype(F32))
            ms = jnp.mean(y * y, axis=-1, keepdims=True)
            o_ref[pl.ds(r0, q), :] = (y * lax.rsqrt(ms + EPS) * nw_ref[...]).astype(o_ref.dtype)

    def run(direction):
        def body(i, carry):
            chunk(direction, i)
            return carry
        lax.fori_loop(0, n_sub, body, 0)

    @pl.when(j < n_steps)
    def _():
        run(0)

    @pl.when(j >= n_steps)
    def _():
        run(1)


def _two_phase_maps(n_steps):
    def both(b, j):
        return (b * n_steps + jnp.where(j < n_steps, j, 2 * n_steps - 1 - j), 0)

    def first(b, j):
        return (b * n_steps + jnp.where(j < n_steps, j, n_steps - 1), 0)

    def second(b, j):
        return (b * n_steps + jnp.where(j < n_steps, n_steps - 1, 2 * n_steps - 1 - j), 0)

    return both, first, second


def _ssd_mixer(z, xbc, dt, prm, dexp, nw, e_mat, *, bsz, seqlen):
    q = SSD_Q
    rows = MIX_SUB * q
    n_steps = seqlen // rows
    tok = bsz * seqlen
    both, _, second = _two_phase_maps(n_steps)

    return pl.pallas_call(
        functools.partial(_ssd_kernel, n_steps=n_steps, n_sub=MIX_SUB, q=q),
        grid=(bsz, 2 * n_steps),
        in_specs=[
            pl.BlockSpec((rows, SSD_WIDTH), second),
            pl.BlockSpec((rows, XBC_WIDTH), both),
            pl.BlockSpec((rows, LANES), both),
            _const_spec(prm.shape),
            _const_spec(dexp.shape),
            _const_spec(nw.shape),
            _const_spec(e_mat.shape),
        ],
        out_specs=pl.BlockSpec((rows, SSD_WIDTH), second),
        out_shape=jax.ShapeDtypeStruct((tok, SSD_WIDTH), BF16),
        scratch_shapes=[
            pltpu.VMEM((seqlen, SSD_WIDTH), F32),
            pltpu.VMEM((SSD_GROUPS, SSD_STATE, SSD_WIDTH // SSD_GROUPS), F32),
        ],
        compiler_params=pltpu.CompilerParams(
            dimension_semantics=("arbitrary", "arbitrary"), vmem_limit_bytes=VMEM_LIMIT),
        name="ssd_mixer",
    )(z, xbc, dt, prm, dexp, nw, e_mat)


HG_MXU_LEVELS = (1, 2)


def _hg_level_mats(q):
    mats = np.zeros((2, 1 + len(HG_MXU_LEVELS), q, q), np.float32)
    t = np.arange(q)[:, None]
    r = np.arange(q)[None, :]
    mats[0, 0] = r <= t
    mats[1, 0] = r >= t
    for i, lev in enumerate(HG_MXU_LEVELS):
        same = (t >> (lev + 1)) == (r >> (lev + 1))
        t_up = ((t >> lev) & 1) == 1
        r_up = ((r >> lev) & 1) == 1
        mats[0, 1 + i] = same & ((t_up & r_up & (r <= t)) | (~t_up & ~r_up & (r > t)))
        mats[1, 1 + i] = same & ((~t_up & ~r_up & (r >= t)) | (t_up & r_up & (r < t)))
    return np.concatenate([mats, mats], axis=3).reshape(2, -1, 2 * q)


def _hg_pair_masks(q):
    n_lev = int(np.log2(q))
    out = np.zeros((2, n_lev + 1, q, q), np.float32)
    t = np.arange(q)[:, None]
    s = np.arange(q)[None, :]
    for lev in range(n_lev):
        same = (t >> (lev + 1)) == (s >> (lev + 1))
        t_up = ((t >> lev) & 1) == 1
        s_up = ((s >> lev) & 1) == 1
        out[0, lev] = same & t_up & ~s_up
        out[1, lev] = same & ~t_up & s_up
    out[:, n_lev] = t == s
    return out


def _hg_kernel(q_ref, ff_ref, fb_ref, iv_ref, g_ref, lb_ref, nw_ref, m_ref, pm_ref, o_ref,
               yf_scr, st_scr, stt_scr, *, n_steps, n_sub, q):
    j = pl.program_id(1)
    n_lev = int(np.log2(q))
    tile = 8

    @pl.when((j == 0) | (j == n_steps))
    def _():
        st_scr[...] = jnp.zeros(st_scr.shape, F32)
        stt_scr[...] = jnp.zeros(stt_scr.shape, BF16)

    def chunk(direction, i):
        fwd = direction == 0
        blk = j if fwd else 2 * n_steps - 1 - j
        r0 = pl.multiple_of((i if fwd else n_sub - 1 - i) * q, q)
        row0 = pl.multiple_of(blk * (n_sub * q) + r0, q)
        rs = pl.ds(r0, q)
        f_ref = ff_ref if fwd else fb_ref
        q_bf = q_ref[rs, :]
        v_bf = iv_ref[rs, :]
        fh = f_ref[rs, :].astype(F32)
        lbf = lb_ref[0, 2 * direction:2 * direction + 1, :]
        oml = lb_ref[0, 2 * direction + 1:2 * direction + 2, :]
        e = jnp.exp(-jnp.abs(fh))
        r = 1.0 / (1.0 + e)
        er = e * r
        pos = fh >= 0.0
        fval = lbf + oml * jnp.where(pos, r, er)
        k_f = oml * jnp.where(pos, er, r)
        l_hi, l_lo = _split2(jnp.log2(fval))
        l_cat = jnp.concatenate([l_hi, l_lo], axis=0)
        sums = _dot(m_ref[direction], l_cat)
        cum = sums[0:q]

        def pick_rows(lev, a, b):
            sub = lax.broadcasted_iota(jnp.int32, (1, tile, HG_DK), 1)
            up = ((sub >> lev) & 1) == 1
            is_query = up if fwd else jnp.logical_not(up)
            shp = (q // tile, tile, HG_DK)
            return jnp.where(is_query, a.reshape(shp), b.reshape(shp)).reshape(q, HG_DK)

        end = q - 1 if fwd else 0
        y_parts = []
        for h in range(HG_HEADS):
            sl = slice(h * HG_DK, (h + 1) * HG_DK)
            q_h = q_bf[:, sl].astype(F32)
            k_h = k_f[:, sl]
            cum_h = cum[:, sl]
            v_h = v_bf[:, sl]

            attn = pm_ref[direction, n_lev] * jnp.sum(q_h * k_h, axis=-1, keepdims=True)
            for lev in range(n_lev):
                if lev == 0:
                    x_l = pick_rows(0, q_h * fval[:, sl], k_h)
                elif lev in HG_MXU_LEVELS:
                    i_l = 1 + HG_MXU_LEVELS.index(lev)
                    x_l = pick_rows(lev, q_h, k_h) * jnp.exp2(sums[i_l * q:(i_l + 1) * q, sl])
                else:
                    half = 1 << lev
                    slabs = []
                    for b0 in range(0, q, 2 * half):
                        lo = slice(b0, b0 + half)
                        hi = slice(b0 + half, b0 + 2 * half)
                        if fwd:
                            ref = cum_h[b0 + half - 1:b0 + half, :]
                            slabs.append(k_h[lo] * jnp.exp2(ref - cum_h[lo]))
                            slabs.append(q_h[hi] * jnp.exp2(cum_h[hi] - ref))
                        else:
                            ref = cum_h[b0 + half:b0 + half + 1, :]
                            slabs.append(q_h[lo] * jnp.exp2(cum_h[lo] - ref))
                            slabs.append(k_h[hi] * jnp.exp2(ref - cum_h[hi]))
                    x_l = jnp.concatenate(slabs, axis=0)
                x_t = x_l.T.astype(BF16)
                attn = attn + pm_ref[direction, lev] * _dot(x_l.astype(BF16), x_t)

            eq = jnp.exp2(cum_h)
            xq = (q_h * eq).astype(BF16)
            xk = (k_h * jnp.exp2(cum_h[end:end + 1, :] - cum_h)).astype(BF16)
            lhs = jnp.concatenate([attn.astype(BF16), xq], axis=1)
            rhs = jnp.concatenate([v_h, stt_scr[h]], axis=0)
            y_parts.append(_dot(lhs, rhs))
            v_t = v_h.astype(F32).T.astype(BF16)
            st_new = st_scr[h] * eq[end:end + 1, :] + _dot(v_t, xk)
            st_scr[h] = st_new
            stt_scr[h] = st_new.T.astype(BF16)
        y = jnp.concatenate(y_parts, axis=1)

        if fwd:
            yf_scr[pl.ds(row0, q), :] = y
        else:
            y = y + yf_scr[pl.ds(row0, q), :]
            gate = _silu(g_ref[rs, :].astype(F32))
            nw = nw_ref[...]
            for h in range(HG_HEADS):
                sl = slice(h * HG_DV, (h + 1) * HG_DV)
                y_h = y[:, sl]
                ms = jnp.mean(y_h * y_h, axis=-1, keepdims=True)
                o_ref[rs, sl] = (y_h * lax.rsqrt(ms + EPS) * nw[:, sl] * gate[:, sl]).astype(o_ref.dtype)

    def run(direction):
        def body(i, carry):
            chunk(direction, i)
            return carry
        lax.fori_loop(0, n_sub, body, 0)

    @pl.when(j < n_steps)
    def _():
        run(0)

    @pl.when(j >= n_steps)
    def _():
        run(1)


def _hg_mixer(qp, ff, fb, iv, g, lb_all, layer, nw, level_mats, pair_masks, *, bsz, seqlen):
    q = HG_Q
    rows = MIX_SUB * q
    n_steps = seqlen // rows
    tok = bsz * seqlen
    both, first, second = _two_phase_maps(n_steps)

    blk = lambda m: pl.BlockSpec((rows, HG_WIDTH), m)
    return pl.pallas_call(
        functools.partial(_hg_kernel, n_steps=n_steps, n_sub=MIX_SUB, q=q),
        grid=(bsz, 2 * n_steps),
        in_specs=[
            blk(both), blk(first), blk(second), blk(both), blk(second),
            pl.BlockSpec((1, 8, HG_WIDTH), lambda b, j: (layer, 0, 0), pipeline_mode=pl.Buffered(1)),
            _const_spec(nw.shape),
            _const_spec(level_mats.shape),
            _const_spec(pair_masks.shape),
        ],
        out_specs=blk(second),
        out_shape=jax.ShapeDtypeStruct((tok, HG_WIDTH), BF16),
        scratch_shapes=[
            pltpu.VMEM((seqlen, HG_WIDTH), F32),
            pltpu.VMEM((HG_HEADS, HG_DV, HG_DK), F32),
            pltpu.VMEM((HG_HEADS, HG_DK, HG_DV), BF16),
        ],
        compiler_params=pltpu.CompilerParams(
            dimension_semantics=("arbitrary", "arbitrary"), vmem_limit_bytes=VMEM_LIMIT),
        name="hgrn2_mixer",
    )(qp, ff, fb, iv, g, lb_all, nw, level_mats, pair_masks)


def _out_mlp_kernel(ys_ref, yh_ref, x_ref, mod_ref, nw_ref, wo_ref, wu_ref, wd_ref, fw_ref, o_ref,
                    *, final):
    d = D_MODEL
    gate1 = mod_ref[0, :, 2 * d:3 * d]
    shift2 = mod_ref[0, :, 3 * d:4 * d]
    scale2 = mod_ref[0, :, 4 * d:5 * d]
    gate2 = mod_ref[0, :, 5 * d:6 * d]
    y = _dot(ys_ref[...], wo_ref[0:SSD_WIDTH, :]) + _dot(yh_ref[...], wo_ref[SSD_WIDTH:, :])
    x1 = x_ref[...] + gate1 * y
    ms = jnp.mean(x1 * x1, axis=-1, keepdims=True)
    h = ((x1 * lax.rsqrt(ms + EPS) * nw_ref[...]) * (1.0 + scale2) + shift2).astype(BF16)
    acc = jnp.zeros(x1.shape, F32)
    nc = 2 * PROJ_NC
    for c in range(0, D_FF, nc):
        u = jnp.maximum(_dot(h, wu_ref[:, c:c + nc]), 0.0)
        acc = acc + _dot((u * u).astype(BF16), wd_ref[c:c + nc, :])
    x2 = x1 + gate2 * acc
    if final:
        ms2 = jnp.mean(x2 * x2, axis=-1, keepdims=True)
        x2 = x2 * lax.rsqrt(ms2 + EPS) * fw_ref[...]
    o_ref[...] = x2


def _out_mlp(ys, yh, x2, mod_l, nw, w_out, w_up, w_down, fw, *, bsz, seqlen, final):
    tm = PROJ_TM
    nt = seqlen // tm
    d = D_MODEL

    def t_map(b, i):
        return (b * nt + i, 0)

    return pl.pallas_call(
        functools.partial(_out_mlp_kernel, final=final),
        grid=(bsz, nt),
        in_specs=[
            pl.BlockSpec((tm, SSD_WIDTH), t_map),
            pl.BlockSpec((tm, HG_WIDTH), t_map),
            pl.BlockSpec((tm, d), t_map),
            pl.BlockSpec((1, 1, N_MOD * d), lambda b, i: (b, 0, 0)),
            _const_spec(nw.shape),
            _const_spec(w_out.shape),
            _const_spec(w_up.shape),
            _const_spec(w_down.shape),
            _const_spec(fw.shape),
        ],
        out_specs=pl.BlockSpec((tm, d), t_map),
        out_shape=jax.ShapeDtypeStruct(x2.shape, F32),
        compiler_params=pltpu.CompilerParams(
            dimension_semantics=("arbitrary", "arbitrary"), vmem_limit_bytes=VMEM_LIMIT),
        name="out_proj_mlp",
    )(ys, yh, x2, mod_l, nw, w_out, w_up, w_down, fw)


def _pad_lanes(v):
    return jnp.pad(v.astype(F32), ((0, 0), (0, LANES - v.shape[1])))


def kernel(x, c, ada_w, ada_b, norm_mix_w, norm_mlp_w, w_in, conv_w, conv_b, dt_bias, a_log, d_skip,
           ssd_norm_w, hg_lower_bounds, hg_norm_w, w_out, w_up, w_down, final_norm_w):
    bsz, seqlen, d = x.shape
    depth = w_in.shape[0]
    assert d == D_MODEL and seqlen % PROJ_TM == 0 and seqlen % SSD_Q == 0 and seqlen % HG_Q == 0

    mod = _modulation(c.astype(F32), ada_w.astype(F32), ada_b.astype(F32))
    mod = mod.reshape(depth, bsz, 1, N_MOD * d)
    lb_all = _lower_bounds(hg_lower_bounds)

    c_dt = SSD_WIDTH + XBC_WIDTH
    w_main = jnp.concatenate([w_in[:, :, :c_dt], w_in[:, :, c_dt + SSD_HEADS:]], axis=2).astype(BF16)
    w_dt = jnp.pad(w_in[:, :, c_dt:c_dt + SSD_HEADS],
                   ((0, 0), (0, 0), (0, LANES - SSD_HEADS))).astype(BF16)
    w_out_b = w_out.astype(BF16)
    w_up_b = w_up.astype(BF16)
    w_down_b = w_down.astype(BF16)

    heads = np.arange(SSD_WIDTH) // SSD_HEAD_DIM
    e_mat = jnp.asarray((np.arange(LANES)[:, None] == heads[None, :]).astype(np.float32), dtype=BF16)
    level_mats = jnp.asarray(_hg_level_mats(HG_Q), dtype=BF16)
    pair_masks = jnp.asarray(_hg_pair_masks(HG_Q), dtype=F32)
    hg_nw = jnp.tile(hg_norm_w.astype(F32), (1, HG_HEADS))
    fw = final_norm_w.astype(F32).reshape(1, d)

    xc = x.astype(F32).reshape(bsz * seqlen, d)
    for l in range(depth):
        z, xbc, dt, qp, ff, fb, iv, g = _in_proj(
            xc, mod[l], norm_mix_w[l].astype(F32).reshape(1, d), w_main[l], w_dt[l],
            conv_w[l].astype(F32), conv_b[l].astype(F32).reshape(1, XBC_WIDTH),
            bsz=bsz, seqlen=seqlen)
        prm = jnp.concatenate([_pad_lanes(dt_bias[l]), _pad_lanes(a_log[l]),
                               jnp.zeros((4, LANES), F32)], axis=0)
        dexp = jnp.repeat(d_skip[l].astype(F32), SSD_HEAD_DIM).reshape(1, SSD_WIDTH)
        ys = _ssd_mixer(z, xbc, dt, prm, dexp, ssd_norm_w[l].astype(F32).reshape(1, SSD_WIDTH),
                        e_mat, bsz=bsz, seqlen=seqlen)
        yh = _hg_mixer(qp, ff, fb, iv, g, lb_all, l, hg_nw[l].reshape(1, HG_WIDTH), level_mats, pair_masks,
                       bsz=bsz, seqlen=seqlen)
        xc = _out_mlp(ys, yh, xc, mod[l], norm_mlp_w[l].astype(F32).reshape(1, d),
                      w_out_b[l], w_up_b[l], w_down_b[l], fw,
                      bsz=bsz, seqlen=seqlen, final=(l == depth - 1))
    return xc.reshape(bsz, seqlen, d).astype(x.dtype)
```

```python
import functools

import numpy as np
import jax
import jax.numpy as jnp
from jax import lax
from jax.experimental import pallas as pl
from jax.experimental.pallas import tpu as pltpu

F32 = jnp.float32
BF16 = jnp.bfloat16

D_MODEL = 1024
N_MOD = 6
SSD_WIDTH = 1024
SSD_HEAD_DIM = 64
SSD_HEADS = 16
SSD_GROUPS = 2
SSD_STATE = 128
SSD_CONV = 5
XBC_WIDTH = SSD_WIDTH + 2 * SSD_GROUPS * SSD_STATE
HG_WIDTH = 1024
HG_HEADS = 8
HG_DK = 128
HG_DV = 128
D_FF = 4096
EPS = 1e-6
LB_FLOOR = 1e-30

LANES = 128
HALO = 16
VMEM_LIMIT = 56 * 1024 * 1024
NEG = -1e30
LOG2E = 1.4426950408889634

PROJ_TM = 512
PROJ_NC = 512
PROJ_PLAIN_NC = 1024
SSD_Q = 128
HG_Q = 128
MIX_ROWS = 1024
HG_UNROLL = 4
SSD_UNROLL = 4

def _sigmoid(v):
    return 1.0 / (1.0 + jnp.exp(-v))


def _silu(v):
    return v * _sigmoid(v)


def _softplus(v):
    return jnp.maximum(v, 0.0) + jnp.log(1.0 + jnp.exp(-jnp.abs(v)))


def _dot(a, b):
    return jnp.dot(a, b, preferred_element_type=F32)


def _dot_nt(a, b):
    return lax.dot_general(a, b, (((1,), (1,)), ((), ())), preferred_element_type=F32)


def _split2(v):
    hi = v.astype(BF16)
    lo = (v - hi.astype(F32)).astype(BF16)
    return hi, lo


def _split3(v):
    hi = v.astype(BF16)
    r = v - hi.astype(F32)
    mid = r.astype(BF16)
    lo = (r - mid.astype(F32)).astype(BF16)
    return hi, mid, lo


def _const_spec(shape):
    nd = len(shape)
    return pl.BlockSpec(shape, lambda *_: (0,) * nd, pipeline_mode=pl.Buffered(1))


def _mod_kernel(c_ref, w_ref, b_ref, o_ref):
    ca = _silu(c_ref[...]).astype(BF16)
    o_ref[0] = _dot(ca, w_ref[0].astype(BF16)) + b_ref[0]


def _modulation(c, ada_w, ada_b):
    depth, d, n = ada_w.shape
    bsz = c.shape[0]
    nb = n // d
    return pl.pallas_call(
        _mod_kernel,
        grid=(depth, nb),
        in_specs=[
            pl.BlockSpec((bsz, d), lambda l, j: (0, 0)),
            pl.BlockSpec((1, d, d), lambda l, j: (l, 0, j)),
            pl.BlockSpec((1, 1, d), lambda l, j: (l, 0, j)),
        ],
        out_specs=pl.BlockSpec((1, bsz, d), lambda l, j: (l, 0, j)),
        out_shape=jax.ShapeDtypeStruct((depth, bsz, n), F32),
        compiler_params=pltpu.CompilerParams(
            dimension_semantics=("arbitrary", "arbitrary"), vmem_limit_bytes=VMEM_LIMIT),
        name="adaln_modulation",
    )(c, ada_w, ada_b.reshape(depth, 1, n))


def _lb_kernel(lb_ref, o_ref):
    depth = lb_ref.shape[1]
    for d in range(2):
        rows = [lb_ref[d, l:l + 1, :] for l in range(depth)]
        m = rows[0]
        for r in rows[1:]:
            m = jnp.maximum(m, r)
        es = [jnp.exp(r - m) for r in rows]
        tot = es[0]
        for e in es[1:]:
            tot = tot + e
        inv = 1.0 / tot
        ps = [e * inv for e in es]
        run = ps[0]
        for l in range(depth):
            if l > 0:
                run = run + ps[l]
            lb = run - ps[0]
            o_ref[l, 2 * d:2 * d + 1, :] = jnp.maximum(lb, LB_FLOOR)
            o_ref[l, 2 * d + 1:2 * d + 2, :] = 1.0 - lb
    o_ref[:, 4:8, :] = jnp.zeros((depth, 4, o_ref.shape[2]), F32)


def _lower_bounds(hg_lower_bounds):
    _, depth, w = hg_lower_bounds.shape
    return pl.pallas_call(
        _lb_kernel,
        out_shape=jax.ShapeDtypeStruct((depth, 8, w), F32),
        name="hgrn2_lower_bounds",
    )(hg_lower_bounds.astype(F32))


def _inproj_kernel(x_ref, xp_ref, xn_ref, mod_ref, nw_ref, w_ref, wdt_ref, cw_ref, cb_ref,
                   z_ref, xbc_ref, dt_ref, q_ref, ff_ref, fb_ref, iv_ref, g_ref,
                   h_scr, xe_scr, *, tm):
    i = pl.program_id(1)
    n_i = pl.num_programs(1)
    d = D_MODEL
    shift = mod_ref[0, :, 0:d]
    scale = mod_ref[0, :, d:2 * d]
    nw = nw_ref[...]

    def norm_mod(xv):
        ms = jnp.mean(xv * xv, axis=-1, keepdims=True)
        y = xv * lax.rsqrt(ms + EPS) * nw
        return (y * (1.0 + scale) + shift).astype(BF16)

    keep_prev = jnp.where(i > 0, 1.0, 0.0).astype(BF16)
    keep_next = jnp.where(i < n_i - 1, 1.0, 0.0).astype(BF16)
    h_scr[0:HALO, :] = norm_mod(xp_ref[...]) * keep_prev
    h_scr[HALO:HALO + tm, :] = norm_mod(x_ref[...])
    h_scr[HALO + tm:2 * HALO + tm, :] = norm_mod(xn_ref[...]) * keep_next

    hm = h_scr[HALO:HALO + tm, :]
    nc = PROJ_NC

    he = h_scr[...]

    pw = PROJ_PLAIN_NC
    plain = [(z_ref, c, c) for c in range(0, SSD_WIDTH, pw)]
    col = SSD_WIDTH + XBC_WIDTH
    for out_ref in (q_ref, ff_ref, fb_ref, iv_ref, g_ref):
        plain += [(out_ref, c, col + c) for c in range(0, HG_WIDTH, pw)]
        col += HG_WIDTH

    def run_plain(n):
        for _ in range(n):
            out_ref, oc, wc = plain.pop(0)
            out_ref[:, oc:oc + pw] = _dot(hm, w_ref[:, wc:wc + pw]).astype(out_ref.dtype)

    n_xbc = XBC_WIDTH // nc
    for g in range(n_xbc):
        c = g * nc
        xe = xe_scr.at[g]
        xe[...] = _dot(he, w_ref[:, SSD_WIDTH + c:SSD_WIDTH + c + nc])
        run_plain(len(plain) // (n_xbc - g + 1))
        acc = jnp.zeros((tm, nc), F32) + cb_ref[:, c:c + nc]
        for k in range(SSD_CONV):
            off = HALO - SSD_CONV // 2 + k
            acc = acc + cw_ref[k:k + 1, c:c + nc] * xe[off:off + tm, :]
        xbc_ref[:, c:c + nc] = _silu(acc).astype(xbc_ref.dtype)
    run_plain(len(plain))

    dt_ref[...] = _dot(hm, wdt_ref[...])


def _in_proj(x2, mod_l, nw, w_main, w_dt, conv_w, conv_b, *, bsz, seqlen):
    tm = PROJ_TM
    nt = seqlen // tm
    hb = tm // HALO
    n_halo_blocks = bsz * seqlen // HALO
    tok = bsz * seqlen
    d = D_MODEL

    def x_map(b, i):
        return (b * nt + i, 0)

    def xp_map(b, i):
        return (jnp.maximum((b * nt + i) * hb - 1, 0), 0)

    def xn_map(b, i):
        return (jnp.minimum((b * nt + i + 1) * hb, n_halo_blocks - 1), 0)

    def tile(width, dtype):
        return pl.BlockSpec((tm, width), x_map), jax.ShapeDtypeStruct((tok, width), dtype)

    outs = [tile(SSD_WIDTH, BF16), tile(XBC_WIDTH, BF16), tile(LANES, F32)] + [tile(HG_WIDTH, BF16)] * 5
    return pl.pallas_call(
        functools.partial(_inproj_kernel, tm=tm),
        grid=(bsz, nt),
        in_specs=[
            pl.BlockSpec((tm, d), x_map),
            pl.BlockSpec((HALO, d), xp_map),
            pl.BlockSpec((HALO, d), xn_map),
            pl.BlockSpec((1, 1, N_MOD * d), lambda b, i: (b, 0, 0)),
            _const_spec((1, d)),
            _const_spec(w_main.shape),
            _const_spec(w_dt.shape),
            _const_spec(conv_w.shape),
            _const_spec(conv_b.shape),
        ],
        out_specs=[o[0] for o in outs],
        out_shape=[o[1] for o in outs],
        scratch_shapes=[
            pltpu.VMEM((tm + 2 * HALO, d), BF16),
            pltpu.VMEM((XBC_WIDTH // PROJ_NC, tm + 2 * HALO, PROJ_NC), F32),
        ],
        compiler_params=pltpu.CompilerParams(
            dimension_semantics=("arbitrary", "arbitrary"), vmem_limit_bytes=VMEM_LIMIT),
        name="in_proj",
    )(x2, x2, x2, mod_l, nw, w_main, w_dt, conv_w, conv_b)


def _ssd_kernel(z_ref, xbc_ref, dt_ref, prm_ref, dexp_ref, nw_ref, e_ref, o_ref,
                yf_scr, st_scr, *, n_steps, n_sub, q):
    j = pl.program_id(1)
    hpg = SSD_HEADS // SSD_GROUPS
    gw = hpg * SSD_HEAD_DIM

    @pl.when((j == 0) | (j == n_steps))
    def _():
        st_scr[...] = jnp.zeros(st_scr.shape, F32)

    def chunk(direction, i):
        blk = j if direction == 0 else 2 * n_steps - 1 - j
        r0 = pl.multiple_of((i if direction == 0 else n_sub - 1 - i) * q, q)
        row0 = pl.multiple_of(blk * (n_sub * q) + r0, q)
        xbc = xbc_ref[pl.ds(r0, q), :]
        x_bf = xbc[:, 0:SSD_WIDTH]
        x_f = x_bf.astype(F32)
        bias = prm_ref[direction:direction + 1, :]
        a_neg = -jnp.exp(prm_ref[2 + direction:3 + direction, :]) * LOG2E
        dtv = _softplus(dt_ref[pl.ds(r0, q), :] + bias)
        a = dtv * a_neg

        rows = lax.broadcasted_iota(jnp.int32, (q, q), 0)
        cols = lax.broadcasted_iota(jnp.int32, (q, q), 1)
        tri = (cols <= rows) if direction == 0 else (cols >= rows)
        tri_bf = jnp.where(tri, 1.0, 0.0).astype(BF16)
        cum = _dot(jnp.concatenate([tri_bf] * 3, axis=1), jnp.concatenate(_split3(a), axis=0))
        end = q - 1 if direction == 0 else 0
        cum_end = cum[end:end + 1, :]
        ecum = jnp.exp2(cum)
        wst = jnp.exp2(cum_end - cum) * dtv
        cum_t = cum.T
        dt_t = dtv.T

        e_mat = e_ref[...]
        ecum_x = _dot(jnp.concatenate(_split2(ecum), axis=1), jnp.concatenate([e_mat, e_mat], axis=0))
        wst_x = _dot(wst.astype(BF16), e_mat)
        xw = (x_f * wst_x).astype(BF16)

        lane = lax.broadcasted_iota(jnp.int32, (q, LANES), 1)
        low_half = lane < SSD_HEAD_DIM

        y_parts = []
        for g in range(SSD_GROUPS):
            b_bf = xbc[:, SSD_WIDTH + g * SSD_STATE:SSD_WIDTH + (g + 1) * SSD_STATE]
            c0 = SSD_WIDTH + SSD_GROUPS * SSD_STATE + g * SSD_STATE
            c_bf = xbc[:, c0:c0 + SSD_STATE]
            b_t = b_bf.astype(F32).T.astype(BF16)
            scores = _dot(c_bf, b_t)

            st = st_scr[g]
            y_inter = _dot(c_bf, st.astype(BF16)) * ecum_x[:, g * gw:(g + 1) * gw]
            dec = ecum_x[end:end + 1, g * gw:(g + 1) * gw]
            st_scr[g] = st * dec + _dot(b_t, xw[:, g * gw:(g + 1) * gw])

            for p in range(hpg // 2):
                ws = []
                for h in (g * hpg + 2 * p, g * hpg + 2 * p + 1):
                    seg = cum[:, h:h + 1] - cum_t[h:h + 1, :]
                    decay = jnp.exp2(jnp.where(tri, seg, NEG))
                    ws.append((scores * decay * dt_t[h:h + 1, :]).astype(BF16))
                w_pair = jnp.concatenate(ws, axis=1)
                l0 = g * gw + p * LANES
                xp = x_bf[:, l0:l0 + LANES]
                zero = jnp.zeros_like(xp)
                x_bd = jnp.concatenate([jnp.where(low_half, xp, zero),
                                        jnp.where(low_half, zero, xp)], axis=0)
                y_parts.append(_dot(w_pair, x_bd) + y_inter[:, p * LANES:(p + 1) * LANES])
        y = jnp.concatenate(y_parts, axis=1)

        if direction == 0:
            yf_scr[pl.ds(row0, q), :] = y + x_f * dexp_ref[...]
        else:
            y = y + yf_scr[pl.ds(row0, q), :]
            y = y * _silu(z_ref[pl.ds(r0, q), :].astype(F32))
            ms = jnp.mean(y * y, axis=-1, keepdims=True)
            o_ref[pl.ds(r0, q), :] = (y * lax.rsqrt(ms + EPS) * nw_ref[...]).astype(o_ref.dtype)

    def run(direction):
        def body(i, carry):
            chunk(direction, i)
            return carry
        lax.fori_loop(0, n_sub, body, 0, unroll=SSD_UNROLL)

    @pl.when(j < n_steps)
    def _():
        run(0)

    @pl.when(j >= n_steps)
    def _():
        run(1)


def _two_phase_maps(n_steps):
    def both(b, j):
        return (b * n_steps + jnp.where(j < n_steps, j, 2 * n_steps - 1 - j), 0)

    def first(b, j):
        return (b * n_steps + jnp.where(j < n_steps, j, n_steps - 1), 0)

    def second(b, j):
        return (b * n_steps + jnp.where(j < n_steps, n_steps - 1, 2 * n_steps - 1 - j), 0)

    return both, first, second


def _ssd_mixer(z, xbc, dt, prm, dexp, nw, e_mat, *, bsz, seqlen):
    q = SSD_Q
    n_sub = MIX_ROWS // q
    rows = MIX_ROWS
    n_steps = seqlen // rows
    tok = bsz * seqlen
    both, _, second = _two_phase_maps(n_steps)

    return pl.pallas_call(
        functools.partial(_ssd_kernel, n_steps=n_steps, n_sub=n_sub, q=q),
        grid=(bsz, 2 * n_steps),
        in_specs=[
            pl.BlockSpec((rows, SSD_WIDTH), second),
            pl.BlockSpec((rows, XBC_WIDTH), both),
            pl.BlockSpec((rows, LANES), both),
            _const_spec(prm.shape),
            _const_spec(dexp.shape),
            _const_spec(nw.shape),
            _const_spec(e_mat.shape),
        ],
        out_specs=pl.BlockSpec((rows, SSD_WIDTH), second),
        out_shape=jax.ShapeDtypeStruct((tok, SSD_WIDTH), BF16),
        scratch_shapes=[
            pltpu.VMEM((seqlen, SSD_WIDTH), F32),
            pltpu.VMEM((SSD_GROUPS, SSD_STATE, SSD_WIDTH // SSD_GROUPS), F32),
        ],
        compiler_params=pltpu.CompilerParams(
            dimension_semantics=("arbitrary", "arbitrary"), vmem_limit_bytes=VMEM_LIMIT),
        name="ssd_mixer",
    )(z, xbc, dt, prm, dexp, nw, e_mat)


HG_MXU_LEVELS = (1, 2)


def _hg_level_mats(q):
    mats = np.zeros((2, 1 + len(HG_MXU_LEVELS), q, q), np.float32)
    t = np.arange(q)[:, None]
    r = np.arange(q)[None, :]
    mats[0, 0] = r <= t
    mats[1, 0] = r >= t
    for i, lev in enumerate(HG_MXU_LEVELS):
        same = (t >> (lev + 1)) == (r >> (lev + 1))
        t_up = ((t >> lev) & 1) == 1
        r_up = ((r >> lev) & 1) == 1
        mats[0, 1 + i] = same & ((t_up & r_up & (r <= t)) | (~t_up & ~r_up & (r > t)))
        mats[1, 1 + i] = same & ((~t_up & ~r_up & (r >= t)) | (t_up & r_up & (r < t)))
    return np.concatenate([mats, mats], axis=3).reshape(2, -1, 2 * q)


def _hg_pair_masks(q):
    n_lev = int(np.log2(q))
    out = np.zeros((2, n_lev + 1, q, q), np.float32)
    t = np.arange(q)[:, None]
    s = np.arange(q)[None, :]
    for lev in range(n_lev):
        same = (t >> (lev + 1)) == (s >> (lev + 1))
        t_up = ((t >> lev) & 1) == 1
        s_up = ((s >> lev) & 1) == 1
        out[0, lev] = same & t_up & ~s_up
        out[1, lev] = same & ~t_up & s_up
    out[:, n_lev] = t == s
    return out


def _hg_kernel(q_ref, ff_ref, fb_ref, iv_ref, g_ref, lb_ref, nw_ref, m_ref, pm_ref, o_ref,
               yf_scr, st_scr, stt_scr, *, n_steps, n_sub, q):
    j = pl.program_id(1)
    n_lev = int(np.log2(q))
    tile = 8

    @pl.when((j == 0) | (j == n_steps))
    def _():
        st_scr[...] = jnp.zeros(st_scr.shape, F32)
        stt_scr[...] = jnp.zeros(stt_scr.shape, BF16)

    def chunk(direction, i):
        fwd = direction == 0
        blk = j if fwd else 2 * n_steps - 1 - j
        r0 = pl.multiple_of((i if fwd else n_sub - 1 - i) * q, q)
        row0 = pl.multiple_of(blk * (n_sub * q) + r0, q)
        rs = pl.ds(r0, q)
        f_ref = ff_ref if fwd else fb_ref
        q_bf = q_ref[rs, :]
        v_bf = iv_ref[rs, :]
        fh = f_ref[rs, :].astype(F32)
        lbf = lb_ref[0, 2 * direction:2 * direction + 1, :]
        oml = lb_ref[0, 2 * direction + 1:2 * direction + 2, :]
        e = jnp.exp(-jnp.abs(fh))
        r = 1.0 / (1.0 + e)
        er = e * r
        pos = fh >= 0.0
        fval = lbf + oml * jnp.where(pos, r, er)
        k_f = oml * jnp.where(pos, er, r)
        l_hi, l_lo = _split2(jnp.log2(fval))
        l_cat = jnp.concatenate([l_hi, l_lo], axis=0)
        sums = _dot(m_ref[direction], l_cat)
        cum = sums[0:q]

        def pick_rows(lev, a, b):
            sub = lax.broadcasted_iota(jnp.int32, (1, tile, HG_DK), 1)
            up = ((sub >> lev) & 1) == 1
            is_query = up if fwd else jnp.logical_not(up)
            shp = (q // tile, tile, HG_DK)
            return jnp.where(is_query, a.reshape(shp), b.reshape(shp)).reshape(q, HG_DK)

        end = q - 1 if fwd else 0
        y_parts = []
        for h in range(HG_HEADS):
            sl = slice(h * HG_DK, (h + 1) * HG_DK)
            q_h = q_bf[:, sl].astype(F32)
            k_h = k_f[:, sl]
            cum_h = cum[:, sl]
            v_h = v_bf[:, sl]

            attn = pm_ref[direction, n_lev] * jnp.sum(q_h * k_h, axis=-1, keepdims=True)
            for lev in range(n_lev):
                if lev == 0:
                    x_l = pick_rows(0, q_h * fval[:, sl], k_h)
                elif lev in HG_MXU_LEVELS:
                    i_l = 1 + HG_MXU_LEVELS.index(lev)
                    x_l = pick_rows(lev, q_h, k_h) * jnp.exp2(sums[i_l * q:(i_l + 1) * q, sl])
                else:
                    half = 1 << lev
                    slabs = []
                    for b0 in range(0, q, 2 * half):
                        lo = slice(b0, b0 + half)
                        hi = slice(b0 + half, b0 + 2 * half)
                        if fwd:
                            ref = cum_h[b0 + half - 1:b0 + half, :]
                            slabs.append(k_h[lo] * jnp.exp2(ref - cum_h[lo]))
                            slabs.append(q_h[hi] * jnp.exp2(cum_h[hi] - ref))
                        else:
                            ref = cum_h[b0 + half:b0 + half + 1, :]
                            slabs.append(q_h[lo] * jnp.exp2(cum_h[lo] - ref))
                            slabs.append(k_h[hi] * jnp.exp2(ref - cum_h[hi]))
                    x_l = jnp.concatenate(slabs, axis=0)
                x_t = x_l.T.astype(BF16)
                attn = attn + pm_ref[direction, lev] * _dot(x_l.astype(BF16), x_t)

            eq = jnp.exp2(cum_h)
            xq = (q_h * eq).astype(BF16)
            xk = (k_h * jnp.exp2(cum_h[end:end + 1, :] - cum_h)).astype(BF16)
            lhs = jnp.concatenate([xq, attn.astype(BF16)], axis=1)
            rhs = jnp.concatenate([stt_scr[h], v_h], axis=0)
            y_parts.append(_dot(lhs, rhs))
            v_t = v_h.astype(F32).T.astype(BF16)
            st_new = st_scr[h] * eq[end:end + 1, :] + _dot(v_t, xk)
            st_scr[h] = st_new
            stt_scr[h] = st_new.T.astype(BF16)
        y = jnp.concatenate(y_parts, axis=1)

        if fwd:
            yf_scr[pl.ds(row0, q), :] = y
        else:
            y = y + yf_scr[pl.ds(row0, q), :]
            gate = _silu(g_ref[rs, :].astype(F32))
            nw = nw_ref[...]
            for h in range(HG_HEADS):
                sl = slice(h * HG_DV, (h + 1) * HG_DV)
                y_h = y[:, sl]
                ms = jnp.mean(y_h * y_h, axis=-1, keepdims=True)
                o_ref[rs, sl] = (y_h * lax.rsqrt(ms + EPS) * nw[:, sl] * gate[:, sl]).astype(o_ref.dtype)

    def run(direction):
        def body(i, carry):
            chunk(direction, i)
            return carry
        lax.fori_loop(0, n_sub, body, 0, unroll=HG_UNROLL)

    @pl.when(j < n_steps)
    def _():
        run(0)

    @pl.when(j >= n_steps)
    def _():
        run(1)


def _hg_mixer(qp, ff, fb, iv, g, lb_all, layer, nw, level_mats, pair_masks, *, bsz, seqlen):
    q = HG_Q
    n_sub = MIX_ROWS // q
    rows = MIX_ROWS
    n_steps = seqlen // rows
    tok = bsz * seqlen
    both, first, second = _two_phase_maps(n_steps)

    blk = lambda m: pl.BlockSpec((rows, HG_WIDTH), m)
    return pl.pallas_call(
        functools.partial(_hg_kernel, n_steps=n_steps, n_sub=n_sub, q=q),
        grid=(bsz, 2 * n_steps),
        in_specs=[
            blk(both), blk(first), blk(second), blk(both), blk(second),
            pl.BlockSpec((1, 8, HG_WIDTH), lambda b, j: (layer, 0, 0), pipeline_mode=pl.Buffered(1)),
            _const_spec(nw.shape),
            _const_spec(level_mats.shape),
            _const_spec(pair_masks.shape),
        ],
        out_specs=blk(second),
        out_shape=jax.ShapeDtypeStruct((tok, HG_WIDTH), BF16),
        scratch_shapes=[
            pltpu.VMEM((seqlen, HG_WIDTH), F32),
            pltpu.VMEM((HG_HEADS, HG_DV, HG_DK), F32),
            pltpu.VMEM((HG_HEADS, HG_DK, HG_DV), BF16),
        ],
        compiler_params=pltpu.CompilerParams(
            dimension_semantics=("arbitrary", "arbitrary"), vmem_limit_bytes=VMEM_LIMIT),
        name="hgrn2_mixer",
    )(qp, ff, fb, iv, g, lb_all, nw, level_mats, pair_masks)


def _out_mlp_kernel(ys_ref, yh_ref, x_ref, mod_ref, nw_ref, wo_ref, wu_ref, wd_ref, fw_ref, o_ref,
                    *, final):
    d = D_MODEL
    gate1 = mod_ref[0, :, 2 * d:3 * d]
    shift2 = mod_ref[0, :, 3 * d:4 * d]
    scale2 = mod_ref[0, :, 4 * d:5 * d]
    gate2 = mod_ref[0, :, 5 * d:6 * d]
    y = _dot(ys_ref[...], wo_ref[0:SSD_WIDTH, :]) + _dot(yh_ref[...], wo_ref[SSD_WIDTH:, :])
    x1 = x_ref[...] + gate1 * y
    ms = jnp.mean(x1 * x1, axis=-1, keepdims=True)
    h = ((x1 * lax.rsqrt(ms + EPS) * nw_ref[...]) * (1.0 + scale2) + shift2).astype(BF16)
    acc = jnp.zeros(x1.shape, F32)
    nc = 2 * PROJ_NC
    for c in range(0, D_FF, nc):
        u = jnp.maximum(_dot(h, wu_ref[:, c:c + nc]), 0.0)
        acc = acc + _dot((u * u).astype(BF16), wd_ref[c:c + nc, :])
    x2 = x1 + gate2 * acc
    if final:
        ms2 = jnp.mean(x2 * x2, axis=-1, keepdims=True)
        x2 = x2 * lax.rsqrt(ms2 + EPS) * fw_ref[...]
    o_ref[...] = x2


def _out_mlp(ys, yh, x2, mod_l, nw, w_out, w_up, w_down, fw, *, bsz, seqlen, final):
    tm = PROJ_TM
    nt = seqlen // tm
    d = D_MODEL

    def t_map(b, i):
        return (b * nt + i, 0)

    return pl.pallas_call(
        functools.partial(_out_mlp_kernel, final=final),
        grid=(bsz, nt),
        in_specs=[
            pl.BlockSpec((tm, SSD_WIDTH), t_map),
            pl.BlockSpec((tm, HG_WIDTH), t_map),
            pl.BlockSpec((tm, d), t_map),
            pl.BlockSpec((1, 1, N_MOD * d), lambda b, i: (b, 0, 0)),
            _const_spec(nw.shape),
            _const_spec(w_out.shape),
            _const_spec(w_up.shape),
            _const_spec(w_down.shape),
            _const_spec(fw.shape),
        ],
        out_specs=pl.BlockSpec((tm, d), t_map),
        out_shape=jax.ShapeDtypeStruct(x2.shape, F32),
        compiler_params=pltpu.CompilerParams(
            dimension_semantics=("arbitrary", "arbitrary"), vmem_limit_bytes=VMEM_LIMIT),
        name="out_proj_mlp",
    )(ys, yh, x2, mod_l, nw, w_out, w_up, w_down, fw)


def _pad_lanes(v):
    return jnp.pad(v.astype(F32), ((0, 0), (0, LANES - v.shape[1])))


def kernel(x, c, ada_w, ada_b, norm_mix_w, norm_mlp_w, w_in, conv_w, conv_b, dt_bias, a_log, d_skip,
           ssd_norm_w, hg_lower_bounds, hg_norm_w, w_out, w_up, w_down, final_norm_w):
    bsz, seqlen, d = x.shape
    depth = w_in.shape[0]
    assert d == D_MODEL and seqlen % PROJ_TM == 0 and seqlen % MIX_ROWS == 0
    assert MIX_ROWS % SSD_Q == 0 and MIX_ROWS % HG_Q == 0

    mod = _modulation(c.astype(F32), ada_w.astype(F32), ada_b.astype(F32))
    mod = mod.reshape(depth, bsz, 1, N_MOD * d)
    lb_all = _lower_bounds(hg_lower_bounds)

    c_dt = SSD_WIDTH + XBC_WIDTH
    w_in_b = w_in.astype(BF16)
    w_main = jnp.concatenate([w_in_b[:, :, :c_dt], w_in_b[:, :, c_dt + SSD_HEADS:]], axis=2)
    w_dt = jnp.pad(w_in_b[:, :, c_dt:c_dt + SSD_HEADS], ((0, 0), (0, 0), (0, LANES - SSD_HEADS)))
    w_out_b = w_out.astype(BF16)
    w_up_b = w_up.astype(BF16)
    w_down_b = w_down.astype(BF16)

    heads = np.arange(SSD_WIDTH) // SSD_HEAD_DIM
    e_mat = jnp.asarray((np.arange(LANES)[:, None] == heads[None, :]).astype(np.float32), dtype=BF16)
    level_mats = jnp.asarray(_hg_level_mats(HG_Q), dtype=BF16)
    pair_masks = jnp.asarray(_hg_pair_masks(HG_Q), dtype=F32)
    hg_nw = jnp.tile(hg_norm_w.astype(F32), (1, HG_HEADS))
    fw = final_norm_w.astype(F32).reshape(1, d)

    xc = x.astype(F32).reshape(bsz * seqlen, d)
    for l in range(depth):
        z, xbc, dt, qp, ff, fb, iv, g = _in_proj(
            xc, mod[l], norm_mix_w[l].astype(F32).reshape(1, d), w_main[l], w_dt[l],
            conv_w[l].astype(F32), conv_b[l].astype(F32).reshape(1, XBC_WIDTH),
            bsz=bsz, seqlen=seqlen)
        prm = jnp.concatenate([_pad_lanes(dt_bias[l]), _pad_lanes(a_log[l]),
                               jnp.zeros((4, LANES), F32)], axis=0)
        dexp = jnp.repeat(d_skip[l].astype(F32), SSD_HEAD_DIM).reshape(1, SSD_WIDTH)
        ys = _ssd_mixer(z, xbc, dt, prm, dexp, ssd_norm_w[l].astype(F32).reshape(1, SSD_WIDTH),
                        e_mat, bsz=bsz, seqlen=seqlen)
        yh = _hg_mixer(qp, ff, fb, iv, g, lb_all, l, hg_nw[l].reshape(1, HG_WIDTH), level_mats, pair_masks,
                       bsz=bsz, seqlen=seqlen)
        xc = _out_mlp(ys, yh, xc, mod[l], norm_mlp_w[l].astype(F32).reshape(1, d),
                      w_out_b[l], w_up_b[l], w_down_b[l], fw,
                      bsz=bsz, seqlen=seqlen, final=(l == depth - 1))
    return xc.reshape(bsz, seqlen, d).astype(x.dtype)
```
